```python
import math, functools
import jax, jax.numpy as jnp
from jax import lax
import numpy as np

D_MODEL = 4096
BATCH = 4
SEQ = 2048
DEPTH = 2
DEC_BATCH = 8
DEC_SEQ = 4
PAST_LEN = 16384
PAGE_SIZE = 128

D_MIX = D_MODEL
W_A = D_MIX // 4
W_B = D_MIX // 4
W_C = D_MIX // 4
W_D = D_MIX - W_A - W_B - W_C
CONV_K = 31
HD_B = 64
H_B = W_B // HD_B
LORA_W = 64
LORA_A = 64
LORA_G = 128
H_C = 8
HD_C = W_C // H_C
QBLK = 128
H_D = 4
DK_D = W_D // (2 * H_D)
DV_D = W_D // H_D
GLA_LORA = 16
GLA_TAU = 16.0
GLA_CHUNK = 16
D_FF = ((8 * D_MODEL // 3 + 255) // 256) * 256
ALPHA = (2.0 * DEPTH) ** 0.25
BETA = (8.0 * DEPTH) ** -0.25
LN_EPS = 1e-5
GN_EPS = 64e-5
N_A = 2 * W_A
N_B = 3 * W_B + LORA_W + LORA_A + LORA_G
N_C = 3 * W_C + H_C
N_D = 2 * H_D * DK_D + 2 * W_D + GLA_LORA
N_IN = N_A + N_B + N_C + N_D
F32 = jnp.float32

kernel_name = 'hybrid_conv_rwkv7_fox_gla_decoder_step'


def _layer_norm(x, g, b, eps=LN_EPS):
    xf = x.astype(F32)
    mu = jnp.mean(xf, -1, keepdims=True)
    var = jnp.mean(jnp.square(xf - mu), -1, keepdims=True)
    return ((xf - mu) * lax.rsqrt(var + eps) * g.astype(F32) + b.astype(F32)).astype(x.dtype)


def _swiglu(x, wg, wu, wd):
    return (jax.nn.silu(x @ wg) * (x @ wu)) @ wd


def _conv_mixer(u, buf, P):
    val, gate = jnp.split(u, 2, axis=-1)
    z = val * jax.nn.sigmoid(gate)
    zz = jnp.concatenate([buf.astype(z.dtype), z], axis=1)
    y = lax.conv_general_dilated(zz, P['conv_w'][:, None, :].astype(z.dtype), (1,), 'VALID',
                                 dimension_numbers=('NWC', 'WIO', 'NWC'),
                                 feature_group_count=W_A) + P['conv_b']
    y = jax.nn.silu(_layer_norm(y, P['conv_ln_g'], P['conv_ln_b']))
    return y, zz[:, -(CONV_K - 1):]


def _rwkv7_mixer(p, shift, wkv, P):
    Bn, T, _ = p.shape
    prev = jnp.concatenate([shift[:, None].astype(p.dtype), p[:, :-1]], axis=1)
    ps = p + (prev - p) * P['rwkv_mu']
    r, k, v, lw, la, lg = jnp.split(ps, [W_B, 2 * W_B, 3 * W_B, 3 * W_B + LORA_W,
                                         3 * W_B + LORA_W + LORA_A], axis=-1)
    w = -jax.nn.softplus(-(P['rwkv_w0'] + jnp.tanh(lw) @ P['rwkv_w2']).astype(F32)) - 0.5
    a = jax.nn.sigmoid((P['rwkv_a0'] + la @ P['rwkv_a2']).astype(F32))
    g = jax.nn.sigmoid(lg) @ P['rwkv_g2']
    heads = lambda t: t.astype(F32).reshape(Bn, T, H_B, HD_B)
    r, k, v, a, w = heads(r), heads(k), heads(v), heads(a), heads(w)
    kk = k * P['rwkv_kk'].astype(F32).reshape(H_B, HD_B)
    kk = kk * lax.rsqrt(jnp.maximum(jnp.sum(jnp.square(kk), -1, keepdims=True), 1e-24))
    k = k * (1.0 + (a - 1.0) * P['rwkv_ka'].astype(F32).reshape(H_B, HD_B))
    decay = jnp.exp(-jnp.exp(w))

    def step(S, inp):
        r_t, k_t, v_t, kk_t, a_t, d_t = inp
        sa = jnp.einsum('bhij,bhj->bhi', S, -kk_t)
        S = (S * d_t[:, :, None, :] + sa[..., None] * (kk_t * a_t)[:, :, None, :]
             + v_t[..., None] * k_t[:, :, None, :])
        return S, jnp.einsum('bhij,bhj->bhi', S, r_t)

    tm = lambda t: jnp.swapaxes(t, 0, 1)
    S, y = lax.scan(step, wkv.astype(F32), (tm(r), tm(k), tm(v), tm(kk), tm(a), tm(decay)))
    y = tm(y)
    m = jnp.mean(y, -1, keepdims=True)
    var = jnp.mean(jnp.square(y - m), -1, keepdims=True)
    y = ((y - m) * lax.rsqrt(var + GN_EPS)).reshape(Bn, T, W_B) * P['rwkv_gn_g'] + P['rwkv_gn_b']
    bonus = jnp.sum(r * k * P['rwkv_rk'].astype(F32), -1, keepdims=True) * v
    y = (y + bonus.reshape(Bn, T, W_B)) * g
    return y.astype(p.dtype), p[:, -1], S


def _fox_prompt(q, k, v, logf):
    Bn, T = q.shape[:2]
    nb = T // QBLK
    scale = HD_C ** -0.5
    c = jnp.cumsum(logf, axis=1).transpose(0, 2, 1)
    qb = q.reshape(Bn, nb, QBLK, H_C, HD_C).swapaxes(0, 1)
    cb = c.reshape(Bn, H_C, nb, QBLK).transpose(2, 0, 1, 3)
    kpos = jnp.arange(T)

    def block(args):
        i, q_i, c_i = args
        s = jnp.einsum('bqhd,bkhd->bhqk', q_i, k, preferred_element_type=F32) * scale
        s = s + c_i[..., None] - c[:, :, None, :]
        qpos = i * QBLK + jnp.arange(QBLK)
        s = jnp.where(kpos[None, :] <= qpos[:, None], s, -jnp.inf)
        p = jax.nn.softmax(s, axis=-1)
        return jnp.einsum('bhqk,bkhd->bqhd', p, v)

    o = lax.map(block, (jnp.arange(nb), qb, cb))
    return o.swapaxes(0, 1).reshape(Bn, T, H_C, HD_C)


def _fox_sample(q, k, v, logf, ck, cv, clf, layer, page_table):
    Bn, T = q.shape[:2]
    L = page_table.shape[1] * ck.shape[2]
    scale = HD_C ** -0.5
    kp = ck[layer, page_table].reshape(Bn, L, H_C, HD_C)
    vp = cv[layer, page_table].reshape(Bn, L, H_C, HD_C)
    lfp = clf[layer, page_table].reshape(Bn, L, H_C).astype(F32)
    d_past = (lax.cumsum(lfp, axis=1, reverse=True) - lfp).transpose(0, 2, 1)
    c_new = jnp.cumsum(logf, axis=1).transpose(0, 2, 1)
    s_past = (jnp.einsum('bqhd,bkhd->bhqk', q, kp, preferred_element_type=F32) * scale
              + c_new[..., None] + d_past[:, :, None, :])
    s_new = (jnp.einsum('bqhd,bkhd->bhqk', q, k, preferred_element_type=F32) * scale
             + c_new[..., None] - c_new[:, :, None, :])
    s_new = jnp.where(jnp.tril(jnp.ones((T, T), bool)), s_new, -jnp.inf)
    p = jax.nn.softmax(jnp.concatenate([s_past, s_new], axis=-1), axis=-1)
    return (jnp.einsum('bhqk,bkhd->bqhd', p[..., :L], vp)
            + jnp.einsum('bhqk,bkhd->bqhd', p[..., L:], v))


def _gla(q, k, v, lga, S0):
    Bn, T = q.shape[:2]
    C = math.gcd(T, GLA_CHUNK)
    n = T // C

    def chunks(t, d):
        return t.astype(F32).reshape(Bn, n, C, H_D, d).transpose(1, 0, 3, 2, 4)

    qc, kc, vc = chunks(q, DK_D), chunks(k, DK_D), chunks(v, DV_D)
    bc = jnp.cumsum(chunks(lga, DK_D), axis=3)
    tri = jnp.tril(jnp.ones((C, C), bool))[:, :, None]

    def step(S, inp):
        q_, k_, v_, b_ = inp
        rel = jnp.exp(jnp.where(tri, b_[:, :, :, None, :] - b_[:, :, None, :, :], -jnp.inf))
        att = jnp.einsum('bhtk,bhsk,bhtsk->bhts', q_, k_, rel)
        b_end = b_[:, :, -1:, :]
        o = (jnp.einsum('bhts,bhsv->bhtv', att, v_)
             + jnp.einsum('bhtk,bhkv->bhtv', q_ * jnp.exp(b_), S))
        S = (S * jnp.exp(b_end[:, :, 0, :, None])
             + jnp.einsum('bhsk,bhsv->bhkv', k_ * jnp.exp(b_end - b_), v_))
        return S, o

    S, o = lax.scan(step, S0.astype(F32), (qc, kc, vc, bc))
    return o.transpose(1, 0, 3, 2, 4).reshape(Bn, T, H_D, DV_D), S


def _token_mixing(h, conv_buf, shift, wkv, gla_S, attend, P):
    Bn, T, _ = h.shape
    u = h @ P['w_in']
    ua, ub, uc, ud = jnp.split(u, [N_A, N_A + N_B, N_A + N_B + N_C], axis=-1)
    ya, conv_new = _conv_mixer(ua, conv_buf, P)
    yb, shift_new, wkv_new = _rwkv7_mixer(ub, shift, wkv, P)
    qc, kc, vc, fc = jnp.split(uc, [W_C, 2 * W_C, 3 * W_C], axis=-1)
    heads = lambda t: t.reshape(Bn, T, H_C, HD_C)
    qc, kc, vc = heads(qc), heads(kc), heads(vc)
    logf = jax.nn.log_sigmoid((fc + P['fox_bf']).astype(F32))
    yc = attend(qc, kc, vc, logf).reshape(Bn, T, W_C).astype(h.dtype)
    nk = H_D * DK_D
    qd, kd, vd, gd, ld = jnp.split(ud, [nk, 2 * nk, 2 * nk + W_D, 2 * nk + 2 * W_D], axis=-1)
    lga = jax.nn.log_sigmoid((ld @ P['gla_aw'] + P['gla_ab']).astype(F32)) / GLA_TAU
    od, gla_new = _gla(qd * DK_D ** -0.5, kd, vd, lga, gla_S)
    od = od * lax.rsqrt(jnp.mean(jnp.square(od), -1, keepdims=True) + LN_EPS)
    yd = (od.reshape(Bn, T, W_D) * P['gla_ng'] * jax.nn.silu(gd.astype(F32))).astype(h.dtype)
    y = jnp.concatenate([ya.astype(h.dtype), yb, yc, yd], axis=-1) @ P['w_o']
    return y, (kc, vc, logf, conv_new, shift_new, wkv_new, gla_new)


def _layer(x, mix, P):
    x = _layer_norm(ALPHA * x + 0.5 * _swiglu(x, P['ffn_wg'][0], P['ffn_wu'][0], P['ffn_wd'][0]),
                    P['ln_g'][0], P['ln_b'][0])
    y, st = mix(x)
    x = _layer_norm(ALPHA * x + y, P['ln_g'][1], P['ln_b'][1])
    x = _layer_norm(ALPHA * x + 0.5 * _swiglu(x, P['ffn_wg'][1], P['ffn_wu'][1], P['ffn_wd'][1]),
                    P['ln_g'][2], P['ln_b'][2])
    return x, st


def setup_inputs(seed: int = 0) -> dict:
    key = jax.random.key(seed)
    keys = jax.random.split(key, 48)
    ctr = [0]

    def nk():
        ctr[0] += 1
        return keys[ctr[0] - 1]

    def nrm(shape, scale=1.0):
        return jax.random.normal(nk(), shape, jnp.float32) * scale

    def unif(shape, lo, hi):
        return jax.random.uniform(nk(), shape, jnp.float32, lo, hi)

    n_pages = PAST_LEN // PAGE_SIZE
    n_pool = (DEC_BATCH * n_pages * 5) // 4
    page_table = jax.random.permutation(nk(), n_pool)[:DEC_BATCH * n_pages].reshape(
        DEC_BATCH, n_pages).astype(jnp.int32)
    return {
        'x_prompt': nrm((BATCH, SEQ, D_MODEL)),
        'x_sample': nrm((DEC_BATCH, DEC_SEQ, D_MODEL)),
        'cache_k': nrm((DEPTH, n_pool, PAGE_SIZE, H_C, HD_C)),
        'cache_v': nrm((DEPTH, n_pool, PAGE_SIZE, H_C, HD_C)),
        'cache_logf': jax.nn.log_sigmoid(3.0 + nrm((DEPTH, n_pool, PAGE_SIZE, H_C))),
        'state_conv': nrm((DEPTH, DEC_BATCH, CONV_K - 1, W_A), 0.5),
        'state_shift': nrm((DEPTH, DEC_BATCH, N_B)),
        'state_wkv': nrm((DEPTH, DEC_BATCH, H_B, HD_B, HD_B), 0.3),
        'state_gla': nrm((DEPTH, DEC_BATCH, H_D, DK_D, DV_D), 0.1),
        'page_table': page_table,
        'ln_g': 1.0 + nrm((DEPTH, 3, D_MODEL), 0.02),
        'ln_b': nrm((DEPTH, 3, D_MODEL), 0.02),
        'ffn_wg': nrm((DEPTH, 2, D_MODEL, D_FF), BETA * D_MODEL ** -0.5),
        'ffn_wu': nrm((DEPTH, 2, D_MODEL, D_FF), BETA * D_MODEL ** -0.5),
        'ffn_wd': nrm((DEPTH, 2, D_FF, D_MODEL), BETA * D_FF ** -0.5),
        'w_in': nrm((DEPTH, D_MODEL, N_IN), D_MODEL ** -0.5),
        'w_o': nrm((DEPTH, D_MIX, D_MODEL), BETA * D_MIX ** -0.5),
        'conv_w': nrm((DEPTH, CONV_K, W_A), CONV_K ** -0.5),
        'conv_b': nrm((DEPTH, W_A), 0.02),
        'conv_ln_g': 1.0 + nrm((DEPTH, W_A), 0.02),
        'conv_ln_b': nrm((DEPTH, W_A), 0.02),
        'rwkv_mu': unif((DEPTH, N_B), 0.0, 1.0),
        'rwkv_w0': unif((DEPTH, W_B), -6.0, -1.0),
        'rwkv_w2': nrm((DEPTH, LORA_W, W_B), 0.1),
        'rwkv_a0': nrm((DEPTH, W_B), 0.5),
        'rwkv_a2': nrm((DEPTH, LORA_A, W_B), 0.1),
        'rwkv_g2': nrm((DEPTH, LORA_G, W_B), LORA_G ** -0.5),
        'rwkv_kk': 0.85 + nrm((DEPTH, W_B), 0.05),
        'rwkv_ka': 1.0 + nrm((DEPTH, W_B), 0.05),
        'rwkv_rk': nrm((DEPTH, H_B, HD_B), 0.1),
        'rwkv_gn_g': 1.0 + nrm((DEPTH, W_B), 0.02),
        'rwkv_gn_b': nrm((DEPTH, W_B), 0.02),
        'fox_bf': 3.0 + nrm((DEPTH, H_C), 0.5),
        'gla_aw': nrm((DEPTH, GLA_LORA, H_D * DK_D), GLA_LORA ** -0.5),
        'gla_ab': nrm((DEPTH, H_D * DK_D), 0.1),
        'gla_ng': 1.0 + nrm((DEPTH, W_D), 0.02),
    }


def reference(x_prompt, x_sample, cache_k, cache_v, cache_logf, state_conv, state_shift,
              state_wkv, state_gla, page_table, ln_g, ln_b, ffn_wg, ffn_wu, ffn_wd, w_in, w_o,
              conv_w, conv_b, conv_ln_g, conv_ln_b, rwkv_mu, rwkv_w0, rwkv_w2, rwkv_a0, rwkv_a2,
              rwkv_g2, rwkv_kk, rwkv_ka, rwkv_rk, rwkv_gn_g, rwkv_gn_b, fox_bf, gla_aw, gla_ab,
              gla_ng):
    def layer_params(l):
        return {'ln_g': ln_g[l], 'ln_b': ln_b[l], 'ffn_wg': ffn_wg[l], 'ffn_wu': ffn_wu[l],
                'ffn_wd': ffn_wd[l], 'w_in': w_in[l], 'w_o': w_o[l], 'conv_w': conv_w[l],
                'conv_b': conv_b[l], 'conv_ln_g': conv_ln_g[l], 'conv_ln_b': conv_ln_b[l],
                'rwkv_mu': rwkv_mu[l], 'rwkv_w0': rwkv_w0[l], 'rwkv_w2': rwkv_w2[l],
                'rwkv_a0': rwkv_a0[l], 'rwkv_a2': rwkv_a2[l], 'rwkv_g2': rwkv_g2[l],
                'rwkv_kk': rwkv_kk[l], 'rwkv_ka': rwkv_ka[l], 'rwkv_rk': rwkv_rk[l],
                'rwkv_gn_g': rwkv_gn_g[l], 'rwkv_gn_b': rwkv_gn_b[l], 'fox_bf': fox_bf[l],
                'gla_aw': gla_aw[l], 'gla_ab': gla_ab[l], 'gla_ng': gla_ng[l]}

    bp, dt = x_prompt.shape[0], x_prompt.dtype
    conv0 = jnp.zeros((bp, CONV_K - 1, W_A), dt)
    shift0 = jnp.zeros((bp, N_B), dt)
    wkv0 = jnp.zeros((bp, H_B, HD_B, HD_B), dt)
    gla0 = jnp.zeros((bp, H_D, DK_D, DV_D), dt)
    xp, xs = x_prompt, x_sample
    st_p, st_s = [], []
    for l in range(DEPTH):
        P = layer_params(l)
        xp, s = _layer(xp, lambda h: _token_mixing(h, conv0, shift0, wkv0, gla0, _fox_prompt, P), P)
        st_p.append(s)
        attend = functools.partial(_fox_sample, ck=cache_k, cv=cache_v, clf=cache_logf,
                                   layer=l, page_table=page_table)
        xs, s = _layer(xs, lambda h: _token_mixing(h, state_conv[l], state_shift[l], state_wkv[l],
                                                   state_gla[l], attend, P), P)
        st_s.append(s)
    stk = lambda sts, i: jnp.stack([s[i] for s in sts])
    return (xp, xs,
            stk(st_p, 0), stk(st_p, 1), stk(st_p, 2), stk(st_p, 3), stk(st_p, 4), stk(st_p, 5), stk(st_p, 6),
            stk(st_s, 0), stk(st_s, 1), stk(st_s, 2), stk(st_s, 3), stk(st_s, 4), stk(st_s, 5), stk(st_s, 6))
```

```python
import functools

import numpy as np
import jax
import jax.numpy as jnp
from jax import lax
from jax.experimental import pallas as pl
from jax.experimental.pallas import tpu as pltpu

F32 = jnp.float32
BF16 = jnp.bfloat16
HI = lax.Precision.HIGHEST

LANES = 128
VMEM_LIMIT = 56 * 1024 * 1024

LN_EPS = 1e-5
GN_EPS = 64e-5
CONV_K = 31
HD_B = 64
H_C = 8
H_D = 4
GLA_TAU = 16.0
GLA_CHUNK = 16
PAGE = 128
RW_TILE = 128
RW_SUB = 16


def _cp(sem):
    return pltpu.CompilerParams(dimension_semantics=sem, vmem_limit_bytes=VMEM_LIMIT)


def _pick(n, prefs):
    for p in prefs:
        if n % p == 0:
            return p
    return n


def _log_sigmoid(x):
    return jnp.minimum(x, 0.0) - jnp.log1p(jnp.exp(-jnp.abs(x)))


def _softplus(x):
    return jnp.maximum(x, 0.0) + jnp.log1p(jnp.exp(-jnp.abs(x)))


def _split_dot(x, w_bf16, n):
    acc = None
    rem = x
    for i in range(n):
        h = rem.astype(BF16)
        d = jnp.dot(h, w_bf16, preferred_element_type=F32)
        acc = d if acc is None else acc + d
        if i < n - 1:
            rem = rem - h.astype(F32)
    return acc


def _iota2(shape, dim):
    return lax.broadcasted_iota(jnp.int32, shape, dim)


def _mm_kernel(x_ref, w_ref, o_ref, acc_ref, *, nk):
    part = jnp.dot(x_ref[...].astype(BF16), w_ref[...].astype(BF16), preferred_element_type=F32)
    if nk == 1:
        o_ref[...] = part.astype(o_ref.dtype)
        return
    k = pl.program_id(2)

    @pl.when(k == 0)
    def _():
        acc_ref[...] = part

    @pl.when(k > 0)
    def _():
        acc_ref[...] += part

    @pl.when(k == nk - 1)
    def _():
        o_ref[...] = acc_ref[...].astype(o_ref.dtype)


def _matmul(x, w, widx, tm, tn, tk, out_dtype=F32):
    M, K = x.shape
    N = w.shape[-1]
    nk = K // tk
    return pl.pallas_call(
        functools.partial(_mm_kernel, nk=nk),
        grid=(M // tm, N // tn, nk),
        in_specs=[pl.BlockSpec((tm, tk), lambda i, j, k: (i, k)),
                  pl.BlockSpec((None,) * len(widx) + (tk, tn), lambda i, j, k: widx + (k, j))],
        out_specs=pl.BlockSpec((tm, tn), lambda i, j, k: (i, j)),
        out_shape=jax.ShapeDtypeStruct((M, N), out_dtype),
        scratch_shapes=[pltpu.VMEM((tm, tn) if nk > 1 else (8, LANES), F32)],
        compiler_params=_cp(("parallel", "parallel", "arbitrary")),
        name="matmul",
    )(x, w)


def _ffn_up_kernel(x_ref, wg_ref, wu_ref, o_ref):
    x = x_ref[...]
    g = jnp.dot(x, wg_ref[...].astype(BF16), preferred_element_type=F32)
    u = jnp.dot(x, wu_ref[...].astype(BF16), preferred_element_type=F32)
    o_ref[...] = (g * jax.nn.sigmoid(g) * u).astype(o_ref.dtype)


def _ffn_up(xb, wg, wu, widx, tm, tn):
    M, K = xb.shape
    N = wg.shape[-1]
    wspec = pl.BlockSpec((None,) * len(widx) + (K, tn), lambda i, j: widx + (0, j))
    return pl.pallas_call(
        _ffn_up_kernel,
        grid=(M // tm, N // tn),
        in_specs=[pl.BlockSpec((tm, K), lambda i, j: (i, 0)), wspec, wspec],
        out_specs=pl.BlockSpec((tm, tn), lambda i, j: (i, j)),
        out_shape=jax.ShapeDtypeStruct((M, N), BF16),
        compiler_params=_cp(("parallel", "arbitrary")),
        name="ffn_up",
    )(xb, wg, wu)


def _add_ln_kernel(x_ref, y_ref, g_ref, b_ref, of_ref, ob_ref, *, alpha, c):
    z = alpha * x_ref[...] + c * y_ref[...]
    mu = jnp.mean(z, axis=-1, keepdims=True)
    zc = z - mu
    var = jnp.mean(zc * zc, axis=-1, keepdims=True)
    o = zc * lax.rsqrt(var + LN_EPS) * g_ref[...] + b_ref[...]
    of_ref[...] = o
    ob_ref[...] = o.astype(BF16)


def _add_ln(x, y, g, b, alpha, c):
    M, D = x.shape
    tr = _pick(M, (256, 32))
    return pl.pallas_call(
        functools.partial(_add_ln_kernel, alpha=alpha, c=c),
        grid=(M // tr,),
        in_specs=[pl.BlockSpec((tr, D), lambda i: (i, 0)),
                  pl.BlockSpec((tr, D), lambda i: (i, 0)),
                  pl.BlockSpec((1, D), lambda i: (0, 0)),
                  pl.BlockSpec((1, D), lambda i: (0, 0))],
        out_specs=[pl.BlockSpec((tr, D), lambda i: (i, 0)),
                   pl.BlockSpec((tr, D), lambda i: (i, 0))],
        out_shape=[jax.ShapeDtypeStruct((M, D), F32), jax.ShapeDtypeStruct((M, D), BF16)],
        compiler_params=_cp(("parallel",)),
        name="add_ln",
    )(x, y, g.reshape(1, D), b.reshape(1, D))


def _conv_kernel(val_ref, gate_ref, buf_ref, w_ref, cb_ref, g_ref, b_ref, y_ref, new_ref, zz_ref,
                 *, tt, nt):
    j = pl.program_id(1)
    hist = CONV_K - 1

    @pl.when(j == 0)
    def _():
        zz_ref[2:2 + hist, :] = buf_ref[0]

    zz_ref[32:32 + tt, :] = val_ref[0] * jax.nn.sigmoid(gate_ref[0])
    rc = min(tt, 32)
    for r0 in range(0, tt, rc):
        acc = None
        for k in range(CONV_K):
            term = w_ref[k:k + 1, :] * zz_ref[2 + k + r0:2 + k + r0 + rc, :]
            acc = term if acc is None else acc + term
        y = acc + cb_ref[...]
        mu = jnp.mean(y, axis=-1, keepdims=True)
        yc = y - mu
        var = jnp.mean(yc * yc, axis=-1, keepdims=True)
        yn = yc * lax.rsqrt(var + LN_EPS) * g_ref[...] + b_ref[...]
        y_ref[0, r0:r0 + rc, :] = (yn * jax.nn.sigmoid(yn)).astype(y_ref.dtype)
    tail = zz_ref[2 + tt:2 + tt + hist, :]
    zz_ref[2:2 + hist, :] = tail

    @pl.when(j == nt - 1)
    def _():
        new_ref[0] = tail


def _conv_mixer(u3, buf, cw, cb, lg, lb):
    B, T, _ = u3.shape
    W = buf.shape[-1]
    tt = _pick(T, (256,))
    nt = T // tt
    return pl.pallas_call(
        functools.partial(_conv_kernel, tt=tt, nt=nt),
        grid=(B, nt),
        in_specs=[pl.BlockSpec((1, tt, W), lambda b, j: (b, j, 0)),
                  pl.BlockSpec((1, tt, W), lambda b, j: (b, j, 1)),
                  pl.BlockSpec((1, CONV_K - 1, W), lambda b, j: (b, 0, 0)),
                  pl.BlockSpec((CONV_K, W), lambda b, j: (0, 0)),
                  pl.BlockSpec((1, W), lambda b, j: (0, 0)),
                  pl.BlockSpec((1, W), lambda b, j: (0, 0)),
                  pl.BlockSpec((1, W), lambda b, j: (0, 0))],
        out_specs=[pl.BlockSpec((1, tt, W), lambda b, j: (b, j, 0)),
                   pl.BlockSpec((1, CONV_K - 1, W), lambda b, j: (b, 0, 0))],
        out_shape=[jax.ShapeDtypeStruct((B, T, W), BF16),
                   jax.ShapeDtypeStruct((B, CONV_K - 1, W), F32)],
        scratch_shapes=[pltpu.VMEM((32 + tt, W), F32)],
        compiler_params=_cp(("parallel", "arbitrary")),
        name="conv_mixer",
    )(u3, u3, buf, cw, cb.reshape(1, W), lg.reshape(1, W), lb.reshape(1, W))


def _rwkv_kernel(r_ref, k_ref, v_ref, lwa_ref, lg_ref, shift_ref, s0_ref, mu_ref, w0_ref, w2_ref,
                 a0_ref, a2_ref, g2_ref, kkp_ref, ka_ref, rk_ref, gng_ref, gnb_ref, sel_ref,
                 y_ref, sout_ref,
                 s_ref, prev_ref, kk_s, d_s, bv_s, k2_s, rp_s, lh_s, vbuf_s, ybuf_s,
                 *, nt, t_valid):
    j = pl.program_id(1)
    TT = RW_TILE
    WB = r_ref.shape[-1]
    npair = WB // LANES

    @pl.when(j == 0)
    def _():
        s_ref[...] = s0_ref[0]
        prev_ref[...] = shift_ref[0]

    ob = (_iota2((LANES, LANES), 0) // HD_B == _iota2((LANES, LANES), 1) // HD_B).astype(BF16)

    def segsum(x, n=3):
        return jnp.concatenate(
            [_split_dot(x[:, LANES * p:LANES * (p + 1)], ob, n) for p in range(npair)], axis=1)

    row0 = _iota2((TT, 1), 0) == 0

    def shifted(x, lo, hi):
        prev = jnp.where(row0, prev_ref[:, lo:hi], pltpu.roll(x, 1, 0))
        return x + (prev - x) * mu_ref[:, lo:hi]

    r_raw, k_raw, v_raw, lwa_raw, lg_raw = r_ref[0], k_ref[0], v_ref[0], lwa_ref[0], lg_ref[0]
    r = shifted(r_raw, 0, WB)
    k = shifted(k_raw, WB, 2 * WB)
    v = shifted(v_raw, 2 * WB, 3 * WB)
    lwa = shifted(lwa_raw, 3 * WB, 3 * WB + LANES)
    lg = shifted(lg_raw, 3 * WB + LANES, 3 * WB + 2 * LANES)
    last = (t_valid if t_valid is not None else TT) - 1
    prev_ref[:, 0:WB] = r_raw[last:last + 1]
    prev_ref[:, WB:2 * WB] = k_raw[last:last + 1]
    prev_ref[:, 2 * WB:3 * WB] = v_raw[last:last + 1]
    prev_ref[:, 3 * WB:3 * WB + LANES] = lwa_raw[last:last + 1]
    prev_ref[:, 3 * WB + LANES:3 * WB + 2 * LANES] = lg_raw[last:last + 1]

    w_pre = w0_ref[...] + jnp.dot(jnp.tanh(lwa), w2_ref[...], precision=HI, preferred_element_type=F32)
    w = -_softplus(-w_pre) - 0.5
    d = jnp.exp(-jnp.exp(w))
    a = jax.nn.sigmoid(a0_ref[...] + jnp.dot(lwa, a2_ref[...], precision=HI, preferred_element_type=F32))
    g = jnp.dot(jax.nn.sigmoid(lg), g2_ref[...], precision=HI, preferred_element_type=F32)
    kk = k * kkp_ref[...]
    kk = kk * lax.rsqrt(jnp.maximum(segsum(kk * kk), 1e-24))
    k2 = k * (1.0 + (a - 1.0) * ka_ref[...])
    bv = kk * a
    rp = d * r - kk * segsum(bv * r)
    rho = segsum(k2 * r)
    rks = segsum(r * k2 * rk_ref[...])
    kk_s[...] = kk
    d_s[...] = d
    bv_s[...] = bv
    k2_s[...] = k2
    rp_s[...] = rp

    vt = v.T
    for p in range(npair):
        lh_s[HD_B * p:HD_B * (p + 1), 0:TT] = vt[LANES * p:LANES * p + HD_B, :]
        lh_s[HD_B * p:HD_B * (p + 1), TT:2 * TT] = vt[LANES * p + HD_B:LANES * (p + 1), :]

    ybuf_s[...] = jnp.zeros_like(ybuf_s)
    lane_c = _iota2((HD_B, LANES), 1) % HD_B
    n_steps = TT if t_valid is None else t_valid
    n_sub = -(-n_steps // RW_SUB)
    for sub in range(n_sub):
        shift = (TT - RW_SUB * sub) % TT
        lh = lh_s[...]
        if shift:
            lh = jnp.concatenate([pltpu.roll(lh[:, 0:TT], shift, 1), pltpu.roll(lh[:, TT:2 * TT], shift, 1)],
                                 axis=1)
        lh_hi = lh.astype(BF16)
        lh_lo = (lh - lh_hi.astype(F32)).astype(BF16)
        for tl in range(RW_SUB):
            sel_t = sel_ref[:, LANES * tl:LANES * (tl + 1)]
            vbuf_s[tl] = (jnp.dot(lh_hi, sel_t, preferred_element_type=F32)
                          + jnp.dot(lh_lo, sel_t, preferred_element_type=F32))

        def step(tl, carry, sub=sub):
            t = RW_SUB * sub + tl
            kk_t = kk_s[pl.ds(t, 1), :]
            d_t = d_s[pl.ds(t, 1), :]
            bv_t = bv_s[pl.ds(t, 1), :]
            k2_t = k2_s[pl.ds(t, 1), :]
            rp_t = rp_s[pl.ds(t, 1), :]
            half = t // HD_B
            mask = lane_c == (t % HD_B)
            for p in range(npair):
                sl = slice(LANES * p, LANES * (p + 1))
                rows = slice(HD_B * p, HD_B * (p + 1))
                S = s_ref[rows, :]
                sa = _split_dot(S * kk_t[:, sl], ob, 2)
                yb = jnp.dot((S * rp_t[:, sl]).astype(BF16), ob, preferred_element_type=F32)
                s_ref[rows, :] = S * d_t[:, sl] - sa * bv_t[:, sl] + vbuf_s[tl, rows, :] * k2_t[:, sl]
                ybuf_s[half, rows, :] = jnp.where(mask, yb, ybuf_s[half, rows, :])
            return carry

        lax.fori_loop(0, min(RW_SUB, n_steps - RW_SUB * sub), step, 0)

    halves = []
    for hf in range(TT // HD_B):
        yt = ybuf_s[hf].T
        cols = []
        for p in range(npair):
            cols.append(yt[0:HD_B, HD_B * p:HD_B * (p + 1)])
            cols.append(yt[HD_B:2 * HD_B, HD_B * p:HD_B * (p + 1)])
        halves.append(jnp.concatenate(cols, axis=1))
    y = jnp.concatenate(halves, axis=0) + v * rho
    m = segsum(y) * (1.0 / HD_B)
    yc = y - m
    var = segsum(yc * yc) * (1.0 / HD_B)
    yn = yc * lax.rsqrt(var + GN_EPS) * gng_ref[...] + gnb_ref[...]
    y_ref[0] = ((yn + rks * v) * g).astype(y_ref.dtype)

    @pl.when(j == nt - 1)
    def _():
        sout_ref[0] = s_ref[...]


def _rwkv_sel():
    rows = np.arange(2 * RW_TILE)
    cols = np.arange(RW_SUB * LANES)
    hit = ((rows[:, None] // RW_TILE) == ((cols[None, :] % LANES) // HD_B)) & \
          ((rows[:, None] % RW_TILE) == (cols[None, :] // LANES))
    return jnp.asarray(hit, dtype=BF16)


def _rwkv_mixer(u3, cb, shift, wkv, P, t_valid=None):
    B, Tp, _ = u3.shape
    WB = P['rwkv_w0'].shape[-1]
    H = WB // HD_B
    npair = H // 2
    nt = Tp // RW_TILE if t_valid is None else 1
    NB = 3 * WB + 2 * LANES
    lb = (cb + 3) * (WB // LANES)
    s0 = wkv.reshape(B, npair, 2, HD_B, HD_B).transpose(0, 1, 3, 2, 4).reshape(B, npair * HD_B, LANES)
    zpad = jnp.zeros((HD_B, WB), F32)
    w2 = jnp.concatenate([P['rwkv_w2'], zpad], axis=0)
    a2 = jnp.concatenate([zpad, P['rwkv_a2']], axis=0)
    row = lambda x: x.reshape(1, -1)
    full = lambda shape: pl.BlockSpec(shape, lambda b, j: (0,) * len(shape))
    tok = lambda w, c: pl.BlockSpec((1, RW_TILE, w), lambda b, j, c=c: (b, j, c))
    y, snew = pl.pallas_call(
        functools.partial(_rwkv_kernel, nt=nt, t_valid=t_valid),
        grid=(B, nt),
        in_specs=[tok(WB, cb), tok(WB, cb + 1), tok(WB, cb + 2), tok(LANES, lb), tok(LANES, lb + 1),
                  pl.BlockSpec((1, 1, NB), lambda b, j: (b, 0, 0)),
                  pl.BlockSpec((1, npair * HD_B, LANES), lambda b, j: (b, 0, 0)),
                  full((1, NB)), full((1, WB)), full((LANES, WB)), full((1, WB)), full((LANES, WB)),
                  full((LANES, WB)), full((1, WB)), full((1, WB)), full((1, WB)), full((1, WB)),
                  full((1, WB)), full((2 * RW_TILE, RW_SUB * LANES))],
        out_specs=[pl.BlockSpec((1, RW_TILE, WB), lambda b, j: (b, j, 0)),
                   pl.BlockSpec((1, npair * HD_B, LANES), lambda b, j: (b, 0, 0))],
        out_shape=[jax.ShapeDtypeStruct((B, nt * RW_TILE, WB), BF16),
                   jax.ShapeDtypeStruct((B, npair * HD_B, LANES), F32)],
        scratch_shapes=[pltpu.VMEM((npair * HD_B, LANES), F32),
                        pltpu.VMEM((1, NB), F32)]
                       + [pltpu.VMEM((RW_TILE, WB), F32)] * 5
                       + [pltpu.VMEM((npair * HD_B, 2 * RW_TILE), F32),
                          pltpu.VMEM((RW_SUB, npair * HD_B, LANES), F32),
                          pltpu.VMEM((RW_TILE // HD_B, npair * HD_B, LANES), F32)],
        compiler_params=_cp(("parallel", "arbitrary")),
        name="rwkv7",
    )(u3, u3, u3, u3, u3, shift.reshape(B, 1, NB), s0, row(P['rwkv_mu']), row(P['rwkv_w0']), w2,
      row(P['rwkv_a0']), a2, P['rwkv_g2'], row(P['rwkv_kk']), row(P['rwkv_ka']), row(P['rwkv_rk']),
      row(P['rwkv_gn_g']), row(P['rwkv_gn_b']), _rwkv_sel())
    snew = snew.reshape(B, npair, HD_B, 2, HD_B).transpose(0, 1, 3, 2, 4).reshape(B, H, HD_B, HD_B)
    return y, snew


def _fox_prep_kernel(f_ref, bf_ref, logf_ref, cq_ref, ck_ref, carry_ref, *, tc):
    j = pl.program_id(1)

    @pl.when(j == 0)
    def _():
        carry_ref[...] = jnp.zeros_like(carry_ref)

    lf = _log_sigmoid(f_ref[0] + bf_ref[...])
    logf_ref[0] = lf
    tri = (_iota2((tc, tc), 0) >= _iota2((tc, tc), 1)).astype(F32)
    c = jnp.dot(tri, lf, precision=HI, preferred_element_type=F32) + carry_ref[...]
    carry_ref[...] = c[tc - 1:tc, :]
    ct = c.T
    for h in range(H_C):
        cq_ref[0, h] = jnp.broadcast_to(c[:, h:h + 1], (tc, LANES))
        ck_ref[0, h] = ct[h:h + 1, :]


def _fox_prep(u3, fcol, bf_pad):
    B, T, _ = u3.shape
    tc = _pick(T, (256,))
    return pl.pallas_call(
        functools.partial(_fox_prep_kernel, tc=tc),
        grid=(B, T // tc),
        in_specs=[pl.BlockSpec((1, tc, LANES), lambda b, j: (b, j, fcol)),
                  pl.BlockSpec((1, LANES), lambda b, j: (0, 0))],
        out_specs=[pl.BlockSpec((1, tc, LANES), lambda b, j: (b, j, 0)),
                   pl.BlockSpec((1, H_C, tc, LANES), lambda b, j: (b, 0, j, 0)),
                   pl.BlockSpec((1, H_C, 1, tc), lambda b, j: (b, 0, 0, j))],
        out_shape=[jax.ShapeDtypeStruct((B, T, LANES), F32),
                   jax.ShapeDtypeStruct((B, H_C, T, LANES), F32),
                   jax.ShapeDtypeStruct((B, H_C, 1, T), F32)],
        scratch_shapes=[pltpu.VMEM((1, LANES), F32)],
        compiler_params=_cp(("parallel", "arbitrary")),
        name="fox_prep",
    )(u3, bf_pad)


def _fox_flash_kernel(q_ref, k_ref, v_ref, cq_ref, ck_ref, o_ref, m_ref, l_ref, acc_ref,
                      *, tq, tk, nk, scale):
    i = pl.program_id(2)
    j = pl.program_id(3)

    @pl.when(j == 0)
    def _():
        m_ref[...] = jnp.full_like(m_ref, -jnp.inf)
        l_ref[...] = jnp.zeros_like(l_ref)
        acc_ref[...] = jnp.zeros_like(acc_ref)

    @pl.when(j * tk <= i * tq + tq - 1)
    def _():
        s = lax.dot_general(q_ref[0].astype(BF16), k_ref[0].astype(BF16), (((1,), (1,)), ((), ())),
                            preferred_element_type=F32) * scale
        rep = tk // LANES
        s = s + jnp.concatenate([cq_ref[0, 0]] * rep, axis=1) - ck_ref[0, 0]
        qpos = i * tq + _iota2((tq, tk), 0)
        kpos = j * tk + _iota2((tq, tk), 1)
        s = jnp.where(kpos <= qpos, s, -jnp.inf)
        m_prev = m_ref[...]
        m_new = jnp.maximum(m_prev, jnp.max(s, axis=1, keepdims=True))
        p = jnp.exp(s - jnp.concatenate([m_new] * rep, axis=1))
        alpha = jnp.exp(m_prev - m_new)
        l_ref[...] = alpha * l_ref[...] + jnp.sum(p, axis=1, keepdims=True)
        acc_ref[...] = alpha * acc_ref[...] + jnp.dot(p.astype(BF16), v_ref[0].astype(BF16),
                                                      preferred_element_type=F32)
        m_ref[...] = m_new

    @pl.when(j == nk - 1)
    def _():
        o_ref[0] = (acc_ref[...] / l_ref[...]).astype(o_ref.dtype)


def _fox_prompt(u3, qcol, cq, ck, hd):
    B, T, _ = u3.shape
    tq = tk = _pick(T, (256,))
    nq, nk = T // tq, T // tk
    kmap = lambda off: (lambda b, h, i, j: (b, jnp.minimum(j, (i * tq + tq - 1) // tk), off + h))
    return pl.pallas_call(
        functools.partial(_fox_flash_kernel, tq=tq, tk=tk, nk=nk, scale=hd ** -0.5),
        grid=(B, H_C, nq, nk),
        in_specs=[pl.BlockSpec((1, tq, hd), lambda b, h, i, j: (b, i, qcol + h)),
                  pl.BlockSpec((1, tk, hd), kmap(qcol + H_C)),
                  pl.BlockSpec((1, tk, hd), kmap(qcol + 2 * H_C)),
                  pl.BlockSpec((1, 1, tq, LANES), lambda b, h, i, j: (b, h, i, 0)),
                  pl.BlockSpec((1, 1, 1, tk),
                               lambda b, h, i, j: (b, h, 0, jnp.minimum(j, (i * tq + tq - 1) // tk)))],
        out_specs=pl.BlockSpec((1, tq, hd), lambda b, h, i, j: (b, i, h)),
        out_shape=jax.ShapeDtypeStruct((B, T, H_C * hd), BF16),
        scratch_shapes=[pltpu.VMEM((tq, LANES), F32), pltpu.VMEM((tq, LANES), F32),
                        pltpu.VMEM((tq, hd), F32)],
        compiler_params=_cp(("parallel", "parallel", "parallel", "arbitrary")),
        name="fox_prompt",
    )(u3, u3, u3, cq, ck)


def _fox_sample_kernel(pt_ref, q_ref, kn_ref, vn_ref, ft_ref, bf_ref, ck_ref, cv_ref, clf_ref,
                       o_ref, lfo_ref, qx_ref, m_ref, l_ref, acc_ref, tail_ref, cnq_ref,
                       *, tn, npg, scale):
    j = pl.program_id(1)
    W = q_ref.shape[-1]
    hd = W // H_C
    R = tn * H_C

    def online(s, vt):
        m_prev = m_ref[...]
        m_new = jnp.maximum(m_prev, jnp.max(s, axis=1, keepdims=True))
        p = jnp.exp(s - m_new)
        alpha = jnp.exp(m_prev - m_new)
        l_ref[...] = alpha * l_ref[...] + jnp.sum(p, axis=1, keepdims=True)
        pv = jnp.dot(p.astype(BF16), vt, preferred_element_type=F32)
        acc_ref[...] = jnp.concatenate([alpha] * (W // LANES), axis=1) * acc_ref[...] + pv
        m_ref[...] = m_new

    @pl.when(j == 0)
    def _():
        q = q_ref[0]
        head_of_lane = _iota2((H_C, W), 1) // hd
        head_of_row = _iota2((H_C, W), 0)
        qx = jnp.concatenate(
            [jnp.where(head_of_lane == head_of_row, jnp.broadcast_to(q[t:t + 1, :], (H_C, W)), 0.0)
             for t in range(tn)], axis=0)
        qx_ref[...] = qx.astype(BF16)
        m_ref[...] = jnp.full_like(m_ref, -jnp.inf)
        l_ref[...] = jnp.zeros_like(l_ref)
        acc_ref[...] = jnp.zeros_like(acc_ref)
        tail_ref[...] = jnp.zeros_like(tail_ref)
        lane = _iota2((H_C, LANES), 1)
        lf = jnp.where(lane < tn, _log_sigmoid(ft_ref[0] + bf_ref[...]), 0.0)
        lfo_ref[0] = lf
        triu = (_iota2((LANES, LANES), 0) <= _iota2((LANES, LANES), 1)).astype(F32)
        ct = jnp.dot(lf, triu, precision=HI, preferred_element_type=F32)
        cnk = jnp.concatenate([ct] * tn, axis=0)
        cnq = jnp.concatenate([jnp.broadcast_to(ct[:, t:t + 1], (H_C, LANES)) for t in range(tn)], axis=0)
        cnq_ref[...] = cnq
        zpad = jnp.zeros((LANES - tn, W), F32)
        kn = jnp.concatenate([kn_ref[0], zpad], axis=0).astype(BF16)
        vn = jnp.concatenate([vn_ref[0], zpad], axis=0).astype(BF16)
        s = lax.dot_general(qx_ref[...], kn, (((1,), (1,)), ((), ())), preferred_element_type=F32) * scale
        s = s + cnq - cnk
        qi = _iota2((R, LANES), 0) // H_C
        s = jnp.where(_iota2((R, LANES), 1) <= qi, s, -jnp.inf)
        online(s, vn)

    lfe = jnp.concatenate([clf_ref[0, 0]] * tn, axis=0)
    later = (_iota2((PAGE, PAGE), 0) > _iota2((PAGE, PAGE), 1)).astype(F32)
    dpast = jnp.dot(lfe, later, precision=HI, preferred_element_type=F32) + tail_ref[...]
    tail_ref[...] = tail_ref[...] + jnp.dot(lfe, jnp.ones((PAGE, PAGE), F32), precision=HI,
                                            preferred_element_type=F32)
    s = lax.dot_general(qx_ref[...], ck_ref[0, 0].astype(BF16), (((1,), (1,)), ((), ())),
                        preferred_element_type=F32) * scale
    online(s + cnq_ref[...] + dpast, cv_ref[0, 0].astype(BF16))

    @pl.when(j == npg - 1)
    def _():
        o = acc_ref[...] / jnp.concatenate([l_ref[...]] * (W // LANES), axis=1)
        keep = (_iota2((R, W), 1) // hd) == (_iota2((R, W), 0) % H_C)
        o = jnp.where(keep, o, 0.0).reshape(tn, H_C, W)
        o_ref[0] = jnp.sum(o, axis=1).astype(o_ref.dtype)


def _fox_sample(q, kn, vn, ft, bf_rep, ck, cv, clf_t, layer, page_table):
    B, tn, W = q.shape
    npg = page_table.shape[1]
    R = tn * H_C
    page = lambda b, j, pt: (layer, pt[b, npg - 1 - j], 0, 0)
    seq = lambda b, j, pt: (b, 0, 0)
    grid_spec = pltpu.PrefetchScalarGridSpec(
        num_scalar_prefetch=1,
        grid=(B, npg),
        in_specs=[pl.BlockSpec((1, tn, W), seq), pl.BlockSpec((1, tn, W), seq), pl.BlockSpec((1, tn, W), seq),
                  pl.BlockSpec((1, H_C, LANES), seq),
                  pl.BlockSpec((H_C, LANES), lambda b, j, pt: (0, 0)),
                  pl.BlockSpec((1, 1, PAGE, W), page), pl.BlockSpec((1, 1, PAGE, W), page),
                  pl.BlockSpec((1, 1, H_C, PAGE), page)],
        out_specs=[pl.BlockSpec((1, tn, W), seq), pl.BlockSpec((1, H_C, LANES), seq)],
        scratch_shapes=[pltpu.VMEM((R, W), BF16), pltpu.VMEM((R, LANES), F32), pltpu.VMEM((R, LANES), F32),
                        pltpu.VMEM((R, W), F32), pltpu.VMEM((R, LANES), F32), pltpu.VMEM((R, LANES), F32)])
    return pl.pallas_call(
        functools.partial(_fox_sample_kernel, tn=tn, npg=npg, scale=(W // H_C) ** -0.5),
        grid_spec=grid_spec,
        out_shape=[jax.ShapeDtypeStruct((B, tn, W), BF16), jax.ShapeDtypeStruct((B, H_C, LANES), F32)],
        compiler_params=_cp(("parallel", "arbitrary")),
        name="fox_sample",
    )(page_table, q, kn, vn, ft, bf_rep, ck, cv, clf_t)


def _gla_kernel(q_ref, k_ref, v_ref, g_ref, ld_ref, aw_ref, ab_ref, ng_ref, s0_ref, y_ref, sout_ref,
                s_ref, o_ref, *, tt, nt, t_valid):
    j = pl.program_id(2)
    dk = q_ref.shape[-1]
    C = GLA_CHUNK
    nch = tt // C

    @pl.when(j == 0)
    def _():
        s_ref[...] = s0_ref[0, 0]

    z = jnp.dot(ld_ref[0], aw_ref[...], precision=HI, preferred_element_type=F32) + ab_ref[...]
    lga = _log_sigmoid(z) * (1.0 / GLA_TAU)
    kx = k_ref[0]
    if t_valid is not None:
        valid = _iota2((tt, dk), 0) < t_valid
        lga = jnp.where(valid, lga, 0.0)
        kx = jnp.where(valid, kx, 0.0)
    same = (_iota2((tt, tt), 0) // C) == (_iota2((tt, tt), 1) // C)
    upto = _iota2((tt, tt), 0) >= _iota2((tt, tt), 1)
    b = jnp.dot((same & upto).astype(F32), lga, precision=HI, preferred_element_type=F32)
    bend = jnp.dot(same.astype(F32), lga, precision=HI, preferred_element_type=F32)
    qx = q_ref[0] * (dk ** -0.5)
    qe = qx * jnp.exp(b)
    kdt = (kx * jnp.exp(bend - b)).T
    dect = jnp.exp(bend).T
    v = v_ref[0]
    rowc = _iota2((tt, v.shape[-1]), 0) // C
    tri_rows = _iota2((C, dk), 0)
    for c in range(nch):
        r0 = C * c
        q_c, k_c, b_c, v_c = qx[r0:r0 + C], kx[r0:r0 + C], b[r0:r0 + C], v[r0:r0 + C]
        o_c = jnp.dot(qe[r0:r0 + C], s_ref[...], precision=HI, preferred_element_type=F32)
        for s in range(C):
            rel = jnp.exp(jnp.where(tri_rows >= s, b_c - b_c[s:s + 1, :], -jnp.inf))
            att = jnp.sum(q_c * k_c[s:s + 1, :] * rel, axis=1, keepdims=True)
            o_c = o_c + att * v_c[s:s + 1, :]
        o_ref[r0:r0 + C, :] = o_c
        s_ref[...] = s_ref[...] * dect[:, r0:r0 + 1] + jnp.dot(
            kdt, jnp.where(rowc == c, v, 0.0), precision=HI, preferred_element_type=F32)
    od = o_ref[...]
    od = od * lax.rsqrt(jnp.mean(od * od, axis=-1, keepdims=True) + LN_EPS)
    gate = g_ref[0]
    y_ref[0] = (od * ng_ref[...] * (gate * jax.nn.sigmoid(gate))).astype(y_ref.dtype)

    @pl.when(j == nt - 1)
    def _():
        sout_ref[0, 0] = s_ref[...]


def _gla_mixer(u3, qc, kc, vc, gc, ldc, aw_pad, ab, ng, s0, t_valid=None):
    B, Tp, _ = u3.shape
    _, H, dk, dv = s0.shape
    tt = _pick(Tp, (256, 128))
    nt = Tp // tt if t_valid is None else 1
    return pl.pallas_call(
        functools.partial(_gla_kernel, tt=tt, nt=nt, t_valid=t_valid),
        grid=(B, H, nt),
        in_specs=[pl.BlockSpec((1, tt, dk), lambda b, h, j: (b, j, qc + h)),
                  pl.BlockSpec((1, tt, dk), lambda b, h, j: (b, j, kc + h)),
                  pl.BlockSpec((1, tt, dv), lambda b, h, j: (b, j, vc + h)),
                  pl.BlockSpec((1, tt, dv), lambda b, h, j: (b, j, gc + h)),
                  pl.BlockSpec((1, tt, LANES), lambda b, h, j: (b, j, ldc)),
                  pl.BlockSpec((LANES, dk), lambda b, h, j: (0, h)),
                  pl.BlockSpec((1, dk), lambda b, h, j: (0, h)),
                  pl.BlockSpec((1, dv), lambda b, h, j: (0, h)),
                  pl.BlockSpec((1, 1, dk, dv), lambda b, h, j: (b, h, 0, 0))],
        out_specs=[pl.BlockSpec((1, tt, dv), lambda b, h, j: (b, j, h)),
                   pl.BlockSpec((1, 1, dk, dv), lambda b, h, j: (b, h, 0, 0))],
        out_shape=[jax.ShapeDtypeStruct((B, nt * tt, H * dv), BF16),
                   jax.ShapeDtypeStruct((B, H, dk, dv), F32)],
        scratch_shapes=[pltpu.VMEM((dk, dv), F32), pltpu.VMEM((tt, dv), F32)],
        compiler_params=_cp(("parallel", "parallel", "arbitrary")),
        name="gla",
    )(u3, u3, u3, u3, u3, aw_pad, ab.reshape(1, -1), ng.reshape(1, -1), s0)


def _pack_w_in(w, wa, wb, wc, wd, dkq):
    D = w.shape[0]
    n_a, n_b = 2 * wa, 3 * wb + 2 * LANES
    c0 = n_a + n_b
    fc = c0 + 3 * wc
    d0 = fc + H_C
    ld0 = d0 + 2 * dkq + 2 * wd
    n_ld = w.shape[1] - ld0
    return jnp.concatenate([w[:, :fc], w[:, d0:ld0], w[:, fc:d0], jnp.zeros((D, LANES - H_C), w.dtype),
                            w[:, ld0:], jnp.zeros((D, LANES - n_ld), w.dtype)], axis=1)


def _token_mixing(xb, B, T, P, layer, st, sample):
    M, D = xb.shape
    wa = wb = wc = D // 4
    wd = D - 3 * (D // 4)
    dkq = wd // 2
    hd = wc // H_C
    n_ld = P['gla_aw'].shape[0]
    w_in = _pack_w_in(P['w_in'], wa, wb, wc, wd, dkq)
    N = w_in.shape[1]
    tm = _pick(M, (1024, 32))
    u = _matmul(xb, w_in, (), tm, _pick(N, (512,)), D)
    u3 = u.reshape(B, T, N)
    cB = 2 * wa
    cC = cB + 3 * wb + 2 * LANES
    cD = cC + 3 * wc
    cF = cD + 2 * dkq + 2 * wd
    cL = cF + LANES
    bf_pad = jnp.concatenate([P['fox_bf'], jnp.zeros((LANES - H_C,), F32)])
    aw_pad = jnp.concatenate([P['gla_aw'], jnp.zeros((LANES - n_ld, dkq), F32)], axis=0)
    kc = u3[:, :, cC + wc:cC + 2 * wc].reshape(B, T, H_C, hd)
    vc = u3[:, :, cC + 2 * wc:cC + 3 * wc].reshape(B, T, H_C, hd)
    shift_new = u3[:, T - 1, cB:cB + 3 * wb + 2 * LANES]

    ya, conv_new = _conv_mixer(u3, st['conv'], P['conv_w'], P['conv_b'], P['conv_ln_g'], P['conv_ln_b'])

    if sample is None:
        up, t_valid = u3, None
    else:
        up, t_valid = jnp.pad(u3, ((0, 0), (0, RW_TILE - T), (0, 0))), T
    yb, wkv_new = _rwkv_mixer(up, cB // wb, st['shift'], st['wkv'], P, t_valid)
    yd, gla_new = _gla_mixer(up, cD // (dkq // H_D), (cD + dkq) // (dkq // H_D),
                             (cD + 2 * dkq) // (wd // H_D), (cD + 2 * dkq + wd) // (wd // H_D),
                             cL // LANES, aw_pad, P['gla_ab'], P['gla_ng'], st['gla'], t_valid)
    yb, yd = yb[:, :T], yd[:, :T]

    if sample is None:
        logf_pad, cq, ck = _fox_prep(u3, cF // LANES, bf_pad.reshape(1, LANES))
        logf = logf_pad[:, :, :H_C]
        yc = _fox_prompt(u3, cC // hd, cq, ck, hd)
    else:
        ft = jnp.pad(u3[:, :, cF:cF + H_C].transpose(0, 2, 1), ((0, 0), (0, 0), (0, LANES - T)))
        bf_rep = jnp.broadcast_to(P['fox_bf'][:, None], (H_C, LANES))
        yc, lft = _fox_sample(u3[:, :, cC:cC + wc], u3[:, :, cC + wc:cC + 2 * wc],
                              u3[:, :, cC + 2 * wc:cC + 3 * wc], ft, bf_rep,
                              sample['ck'], sample['cv'], sample['clf_t'], layer, sample['page_table'])
        logf = lft[:, :, :T].transpose(0, 2, 1)

    ymix = jnp.concatenate([ya, yb, yc, yd], axis=-1).reshape(M, D)
    y = _matmul(ymix, P['w_o_all'], (layer,), tm, _pick(D, (512,)), D)
    return y, (kc, vc, logf, conv_new, shift_new, wkv_new, gla_new)


def _swiglu(xb, P, layer, which):
    M, D = xb.shape
    F = P['ffn_wg_all'].shape[-1]
    widx = (layer, which)
    h = _ffn_up(xb, P['ffn_wg_all'], P['ffn_wu_all'], widx, _pick(M, (1024, 32)), _pick(F, (256,)))
    return _matmul(h, P['ffn_wd_all'], widx, _pick(M, (512, 32)), _pick(D, (256,)), F)


def _layer(x, xb, B, T, P, layer, st, sample, alpha):
    x, xb = _add_ln(x, _swiglu(xb, P, layer, 0), P['ln_g'][0], P['ln_b'][0], alpha, 0.5)
    y, new = _token_mixing(xb, B, T, P, layer, st, sample)
    x, xb = _add_ln(x, y, P['ln_g'][1], P['ln_b'][1], alpha, 1.0)
    x, xb = _add_ln(x, _swiglu(xb, P, layer, 1), P['ln_g'][2], P['ln_b'][2], alpha, 0.5)
    return x, xb, new


def kernel(x_prompt, x_sample, cache_k, cache_v, cache_logf, state_conv, state_shift, state_wkv, state_gla, page_table, ln_g, ln_b, ffn_wg, ffn_wu, ffn_wd, w_in, w_o, conv_w, conv_b, conv_ln_g, conv_ln_b, rwkv_mu, rwkv_w0, rwkv_w2, rwkv_a0, rwkv_a2, rwkv_g2, rwkv_kk, rwkv_ka, rwkv_rk, rwkv_gn_g, rwkv_gn_b, fox_bf, gla_aw, gla_ab, gla_ng):
    big = dict(ffn_wg_all=ffn_wg, ffn_wu_all=ffn_wu, ffn_wd_all=ffn_wd, w_o_all=w_o)
    stacked = dict(ln_g=ln_g, ln_b=ln_b, w_in=w_in, conv_w=conv_w, conv_b=conv_b, conv_ln_g=conv_ln_g, conv_ln_b=conv_ln_b, rwkv_mu=rwkv_mu,
                   rwkv_w0=rwkv_w0, rwkv_w2=rwkv_w2, rwkv_a0=rwkv_a0, rwkv_a2=rwkv_a2, rwkv_g2=rwkv_g2,
                   rwkv_kk=rwkv_kk, rwkv_ka=rwkv_ka, rwkv_rk=rwkv_rk, rwkv_gn_g=rwkv_gn_g,
                   rwkv_gn_b=rwkv_gn_b, fox_bf=fox_bf, gla_aw=gla_aw, gla_ab=gla_ab, gla_ng=gla_ng)
    depth = w_in.shape[0]
    Bp, Tp, D = x_prompt.shape
    Bs, Ts, _ = x_sample.shape
    alpha = (2.0 * depth) ** 0.25
    n_pool, page, hc, hdc = cache_k.shape[1:]
    ck = cache_k.reshape(depth, n_pool, page, hc * hdc)
    cv = cache_v.reshape(depth, n_pool, page, hc * hdc)
    clf_t = cache_logf.transpose(0, 1, 3, 2)
    xp, xs = x_prompt.reshape(Bp * Tp, D), x_sample.reshape(Bs * Ts, D)
    xpb, xsb = xp.astype(BF16), xs.astype(BF16)
    zero_state = dict(conv=jnp.zeros((Bp,) + state_conv.shape[2:], F32),
                      shift=jnp.zeros((Bp,) + state_shift.shape[2:], F32),
                      wkv=jnp.zeros((Bp,) + state_wkv.shape[2:], F32),
                      gla=jnp.zeros((Bp,) + state_gla.shape[2:], F32))
    st_p, st_s = [], []
    for l in range(depth):
        P = {n: a[l] for n, a in stacked.items()}
        P.update(big)
        xp, xpb, s = _layer(xp, xpb, Bp, Tp, P, l, zero_state, None, alpha)
        st_p.append(s)
        st = dict(conv=state_conv[l], shift=state_shift[l], wkv=state_wkv[l], gla=state_gla[l])
        xs, xsb, s = _layer(xs, xsb, Bs, Ts, P, l, st,
                            dict(ck=ck, cv=cv, clf_t=clf_t, page_table=page_table), alpha)
        st_s.append(s)
    stk = lambda sts, i: jnp.stack([s[i] for s in sts])
    return ((xp.reshape(Bp, Tp, D), xs.reshape(Bs, Ts, D))
            + tuple(stk(st_p, i) for i in range(7)) + tuple(stk(st_s, i) for i in range(7)))
```

```python
import functools

import numpy as np
import jax
import jax.numpy as jnp
from jax import lax
from jax.experimental import pallas as pl
from jax.experimental.pallas import tpu as pltpu

F32 = jnp.float32
BF16 = jnp.bfloat16
HI = lax.Precision.HIGHEST

LANES = 128
VMEM_LIMIT = 56 * 1024 * 1024

LN_EPS = 1e-5
GN_EPS = 64e-5
CONV_K = 31
HD_B = 64
H_C = 8
H_D = 4
GLA_TAU = 16.0
GLA_CHUNK = 16
PAGE = 128
RW_TILE = 128
RW_SUB = 16


def _cp(sem):
    return pltpu.CompilerParams(dimension_semantics=sem, vmem_limit_bytes=VMEM_LIMIT)


def _pick(n, prefs):
    for p in prefs:
        if n % p == 0:
            return p
    return n


def _log_sigmoid(x):
    return jnp.minimum(x, 0.0) - jnp.log1p(jnp.exp(-jnp.abs(x)))


def _softplus(x):
    return jnp.maximum(x, 0.0) + jnp.log1p(jnp.exp(-jnp.abs(x)))


def _split_dot(x, w_bf16, n):
    acc = None
    rem = x
    for i in range(n):
        h = rem.astype(BF16)
        d = jnp.dot(h, w_bf16, preferred_element_type=F32)
        acc = d if acc is None else acc + d
        if i < n - 1:
            rem = rem - h.astype(F32)
    return acc


def _iota2(shape, dim):
    return lax.broadcasted_iota(jnp.int32, shape, dim)


def _mm_kernel(x_ref, w_ref, o_ref, acc_ref, *, nk):
    part = jnp.dot(x_ref[...].astype(BF16), w_ref[...].astype(BF16), preferred_element_type=F32)
    if nk == 1:
        o_ref[...] = part.astype(o_ref.dtype)
        return
    k = pl.program_id(2)

    @pl.when(k == 0)
    def _():
        acc_ref[...] = part

    @pl.when(k > 0)
    def _():
        acc_ref[...] += part

    @pl.when(k == nk - 1)
    def _():
        o_ref[...] = acc_ref[...].astype(o_ref.dtype)


def _matmul(x, w, widx, tm, tn, tk, out_dtype=F32):
    M, K = x.shape
    N = w.shape[-1]
    nk = K // tk
    return pl.pallas_call(
        functools.partial(_mm_kernel, nk=nk),
        grid=(M // tm, N // tn, nk),
        in_specs=[pl.BlockSpec((tm, tk), lambda i, j, k: (i, k)),
                  pl.BlockSpec((None,) * len(widx) + (tk, tn), lambda i, j, k: widx + (k, j))],
        out_specs=pl.BlockSpec((tm, tn), lambda i, j, k: (i, j)),
        out_shape=jax.ShapeDtypeStruct((M, N), out_dtype),
        scratch_shapes=[pltpu.VMEM((tm, tn) if nk > 1 else (8, LANES), F32)],
        compiler_params=_cp(("parallel", "parallel", "arbitrary")),
        name="matmul",
    )(x, w)


def _ffn_up_kernel(x_ref, wg_ref, wu_ref, o_ref):
    x = x_ref[...]
    g = jnp.dot(x, wg_ref[...].astype(BF16), preferred_element_type=F32)
    u = jnp.dot(x, wu_ref[...].astype(BF16), preferred_element_type=F32)
    o_ref[...] = (g * jax.nn.sigmoid(g) * u).astype(o_ref.dtype)


def _ffn_up(xb, wg, wu, widx, tm, tn):
    M, K = xb.shape
    N = wg.shape[-1]
    wspec = pl.BlockSpec((None,) * len(widx) + (K, tn), lambda i, j: widx + (0, j))
    return pl.pallas_call(
        _ffn_up_kernel,
        grid=(M // tm, N // tn),
        in_specs=[pl.BlockSpec((tm, K), lambda i, j: (i, 0)), wspec, wspec],
        out_specs=pl.BlockSpec((tm, tn), lambda i, j: (i, j)),
        out_shape=jax.ShapeDtypeStruct((M, N), BF16),
        compiler_params=_cp(("parallel", "arbitrary")),
        name="ffn_up",
    )(xb, wg, wu)


def _add_ln_kernel(x_ref, y_ref, g_ref, b_ref, of_ref, ob_ref, *, alpha, c):
    z = alpha * x_ref[...] + c * y_ref[...]
    mu = jnp.mean(z, axis=-1, keepdims=True)
    zc = z - mu
    var = jnp.mean(zc * zc, axis=-1, keepdims=True)
    o = zc * lax.rsqrt(var + LN_EPS) * g_ref[...] + b_ref[...]
    of_ref[...] = o
    ob_ref[...] = o.astype(BF16)


def _add_ln(x, y, g, b, alpha, c):
    M, D = x.shape
    tr = _pick(M, (256, 32))
    return pl.pallas_call(
        functools.partial(_add_ln_kernel, alpha=alpha, c=c),
        grid=(M // tr,),
        in_specs=[pl.BlockSpec((tr, D), lambda i: (i, 0)),
                  pl.BlockSpec((tr, D), lambda i: (i, 0)),
                  pl.BlockSpec((1, D), lambda i: (0, 0)),
                  pl.BlockSpec((1, D), lambda i: (0, 0))],
        out_specs=[pl.BlockSpec((tr, D), lambda i: (i, 0)),
                   pl.BlockSpec((tr, D), lambda i: (i, 0))],
        out_shape=[jax.ShapeDtypeStruct((M, D), F32), jax.ShapeDtypeStruct((M, D), BF16)],
        compiler_params=_cp(("parallel",)),
        name="add_ln",
    )(x, y, g.reshape(1, D), b.reshape(1, D))


def _conv_kernel(val_ref, gate_ref, buf_ref, w_ref, cb_ref, g_ref, b_ref, y_ref, new_ref, zz_ref,
                 *, tt, nt):
    j = pl.program_id(1)
    hist = CONV_K - 1

    @pl.when(j == 0)
    def _():
        zz_ref[2:2 + hist, :] = buf_ref[0]

    zz_ref[32:32 + tt, :] = val_ref[0] * jax.nn.sigmoid(gate_ref[0])
    rc = min(tt, 32)
    for r0 in range(0, tt, rc):
        acc = None
        for k in range(CONV_K):
            term = w_ref[k:k + 1, :] * zz_ref[2 + k + r0:2 + k + r0 + rc, :]
            acc = term if acc is None else acc + term
        y = acc + cb_ref[...]
        mu = jnp.mean(y, axis=-1, keepdims=True)
        yc = y - mu
        var = jnp.mean(yc * yc, axis=-1, keepdims=True)
        yn = yc * lax.rsqrt(var + LN_EPS) * g_ref[...] + b_ref[...]
        y_ref[0, r0:r0 + rc, :] = (yn * jax.nn.sigmoid(yn)).astype(y_ref.dtype)
    tail = zz_ref[2 + tt:2 + tt + hist, :]
    zz_ref[2:2 + hist, :] = tail

    @pl.when(j == nt - 1)
    def _():
        new_ref[0] = tail


def _conv_mixer(u3, buf, cw, cb, lg, lb):
    B, T, _ = u3.shape
    W = buf.shape[-1]
    tt = _pick(T, (256,))
    nt = T // tt
    return pl.pallas_call(
        functools.partial(_conv_kernel, tt=tt, nt=nt),
        grid=(B, nt),
        in_specs=[pl.BlockSpec((1, tt, W), lambda b, j: (b, j, 0)),
                  pl.BlockSpec((1, tt, W), lambda b, j: (b, j, 1)),
                  pl.BlockSpec((1, CONV_K - 1, W), lambda b, j: (b, 0, 0)),
                  pl.BlockSpec((CONV_K, W), lambda b, j: (0, 0)),
                  pl.BlockSpec((1, W), lambda b, j: (0, 0)),
                  pl.BlockSpec((1, W), lambda b, j: (0, 0)),
                  pl.BlockSpec((1, W), lambda b, j: (0, 0))],
        out_specs=[pl.BlockSpec((1, tt, W), lambda b, j: (b, j, 0)),
                   pl.BlockSpec((1, CONV_K - 1, W), lambda b, j: (b, 0, 0))],
        out_shape=[jax.ShapeDtypeStruct((B, T, W), BF16),
                   jax.ShapeDtypeStruct((B, CONV_K - 1, W), F32)],
        scratch_shapes=[pltpu.VMEM((32 + tt, W), F32)],
        compiler_params=_cp(("parallel", "arbitrary")),
        name="conv_mixer",
    )(u3, u3, buf, cw, cb.reshape(1, W), lg.reshape(1, W), lb.reshape(1, W))


def _rwkv_kernel(r_ref, k_ref, v_ref, lwa_ref, lg_ref, shift_ref, s0_ref, mu_ref, w0_ref, w2h_ref, w2l_ref,
                 a0_ref, a2h_ref, a2l_ref, g2_ref, kkp_ref, ka_ref, rk_ref, gng_ref, gnb_ref, sel_ref,
                 y_ref, sout_ref,
                 s_ref, prev_ref, kk_s, d_s, bv_s, k2_s, rp_s, v_s, rho_s, rks_s, g_s, lhh_s, lhl_s,
                 vbuf_s, ybuf_s, *, G, nt, t_valid):
    j = pl.program_id(1)
    TT = RW_TILE
    WB = r_ref.shape[-1]
    npair = WB // LANES

    @pl.when(j == 0)
    def _():
        s_ref[...] = s0_ref[...]
        prev_ref[...] = shift_ref[...]

    ob = (_iota2((LANES, LANES), 0) // HD_B == _iota2((LANES, LANES), 1) // HD_B).astype(BF16)
    ob2 = jnp.concatenate([ob, ob], axis=0)

    def segsum(x):
        return jnp.concatenate(
            [_split_dot(x[:, LANES * p:LANES * (p + 1)], ob, 2) for p in range(npair)], axis=1)

    def dot3(x, wh_ref, wl_ref):
        xh = x.astype(BF16)
        xl = (x - xh.astype(F32)).astype(BF16)
        wh = wh_ref[...]
        return (jnp.dot(xh, wh, preferred_element_type=F32) + jnp.dot(xl, wh, preferred_element_type=F32)
                + jnp.dot(xh, wl_ref[...], preferred_element_type=F32))

    row0 = _iota2((TT, 1), 0) == 0
    last = (t_valid if t_valid is not None else TT) - 1
    for gi in range(G):
        def shifted(x, lo, hi, gi=gi):
            prev = jnp.where(row0, prev_ref[gi, :, lo:hi], pltpu.roll(x, 1, 0))
            return x + (prev - x) * mu_ref[:, lo:hi]

        raw = (r_ref[gi], k_ref[gi], v_ref[gi], lwa_ref[gi], lg_ref[gi])
        offs = (0, WB, 2 * WB, 3 * WB, 3 * WB + LANES, 3 * WB + 2 * LANES)
        r, k, v, lwa, lg = [shifted(x, offs[i], offs[i + 1]) for i, x in enumerate(raw)]
        for i, x in enumerate(raw):
            prev_ref[gi, :, offs[i]:offs[i + 1]] = x[last:last + 1]

        w = -_softplus(-(w0_ref[...] + dot3(jnp.tanh(lwa), w2h_ref, w2l_ref))) - 0.5
        d = jnp.exp(-jnp.exp(w))
        a = jax.nn.sigmoid(a0_ref[...] + dot3(lwa, a2h_ref, a2l_ref))
        g = jnp.dot(jax.nn.sigmoid(lg).astype(BF16), g2_ref[...], preferred_element_type=F32)
        kk = k * kkp_ref[...]
        kk = kk * lax.rsqrt(jnp.maximum(segsum(kk * kk), 1e-24))
        k2 = k * (1.0 + (a - 1.0) * ka_ref[...])
        bv = kk * a
        kk_s[gi] = kk
        d_s[gi] = d
        bv_s[gi] = bv
        k2_s[gi] = k2
        rp_s[gi] = d * r - kk * segsum(bv * r)
        v_s[gi] = v
        rho_s[gi] = segsum(k2 * r)
        rks_s[gi] = segsum(r * k2 * rk_ref[...])
        g_s[gi] = g

        vt = v.T
        vh = vt.astype(BF16).astype(F32)
        vl = vt - vh
        for p in range(npair):
            for h in range(2):
                src = slice(LANES * p + HD_B * h, LANES * p + HD_B * (h + 1))
                lhh_s[gi, HD_B * p:HD_B * (p + 1), TT * h:TT * (h + 1)] = vh[src, :]
                lhl_s[gi, HD_B * p:HD_B * (p + 1), TT * h:TT * (h + 1)] = vl[src, :]

    ybuf_s[...] = jnp.zeros_like(ybuf_s)
    lane_c = _iota2((HD_B, LANES), 1) % HD_B
    lane_x = _iota2((npair * HD_B, LANES), 1)
    n_steps = TT if t_valid is None else t_valid
    n_sub = -(-n_steps // RW_SUB)
    for sub in range(n_sub):
        base = (TT - RW_SUB * sub) % TT
        for gi in range(G):
            parts = [lhh_s[gi, :, 0:TT], lhh_s[gi, :, TT:2 * TT], lhl_s[gi, :, 0:TT], lhl_s[gi, :, TT:2 * TT]]
            x = 0.0
            for q in range(3, -1, -1):
                sh = (base + RW_SUB * q) % TT
                part = pltpu.roll(parts[q], sh, 1) if sh else parts[q]
                x = jnp.where(lane_x < RW_SUB * (q + 1), part, x)
            xb = x.astype(BF16)
            for t2 in range(RW_SUB // 2):
                res = jnp.dot(xb, sel_ref[:, 2 * LANES * t2:2 * LANES * (t2 + 1)], preferred_element_type=F32)
                vbuf_s[gi, 2 * t2] = res[:, 0:LANES]
                vbuf_s[gi, 2 * t2 + 1] = res[:, LANES:2 * LANES]

        def step(tl, carry, sub=sub):
            t = RW_SUB * sub + tl
            half = t // HD_B
            mask = lane_c == (t % HD_B)
            for gi in range(G):
                kk_t = kk_s[gi, pl.ds(t, 1), :]
                d_t = d_s[gi, pl.ds(t, 1), :]
                bv_t = bv_s[gi, pl.ds(t, 1), :]
                k2_t = k2_s[gi, pl.ds(t, 1), :]
                rp_t = rp_s[gi, pl.ds(t, 1), :]
                for p in range(npair):
                    sl = slice(LANES * p, LANES * (p + 1))
                    rows = slice(HD_B * p, HD_B * (p + 1))
                    S = s_ref[gi, rows, :]
                    p1 = S * kk_t[:, sl]
                    p1h = p1.astype(BF16)
                    p1l = (p1 - p1h.astype(F32)).astype(BF16)
                    sa = jnp.dot(jnp.concatenate([p1h, p1l], axis=1), ob2, preferred_element_type=F32)
                    yb = jnp.dot((S * rp_t[:, sl]).astype(BF16), ob, preferred_element_type=F32)
                    s_ref[gi, rows, :] = (S * d_t[:, sl] - sa * bv_t[:, sl]
                                          + vbuf_s[gi, tl, rows, :] * k2_t[:, sl])
                    ybuf_s[gi, half, rows, :] = jnp.where(mask, yb, ybuf_s[gi, half, rows, :])
            return carry

        lax.fori_loop(0, min(RW_SUB, n_steps - RW_SUB * sub), step, 0, unroll=2)

    for gi in range(G):
        halves = []
        for hf in range(TT // HD_B):
            yt = ybuf_s[gi, hf].T
            cols = []
            for p in range(npair):
                cols.append(yt[0:HD_B, HD_B * p:HD_B * (p + 1)])
                cols.append(yt[HD_B:2 * HD_B, HD_B * p:HD_B * (p + 1)])
            halves.append(jnp.concatenate(cols, axis=1))
        v = v_s[gi]
        y = jnp.concatenate(halves, axis=0) + v * rho_s[gi]
        m = segsum(y) * (1.0 / HD_B)
        yc = y - m
        var = segsum(yc * yc) * (1.0 / HD_B)
        yn = yc * lax.rsqrt(var + GN_EPS) * gng_ref[...] + gnb_ref[...]
        y_ref[gi] = ((yn + rks_s[gi] * v) * g_s[gi]).astype(y_ref.dtype)

    @pl.when(j == nt - 1)
    def _():
        sout_ref[...] = s_ref[...]


def _rwkv_sel():
    rows = np.arange(LANES)
    cols = np.arange(RW_SUB * LANES)
    hit = (rows[:, None] < 4 * RW_SUB) & \
          (((rows[:, None] % (2 * RW_SUB)) // RW_SUB) == ((cols[None, :] % LANES) // HD_B)) & \
          ((rows[:, None] % RW_SUB) == (cols[None, :] // LANES))
    return jnp.asarray(hit, dtype=BF16)


def _rwkv_mixer(u3, cb, shift, wkv, P, t_valid=None):
    B, Tp, _ = u3.shape
    WB = P['rwkv_w0'].shape[-1]
    H = WB // HD_B
    npair = H // 2
    G = 2 if B % 2 == 0 else 1
    nt = Tp // RW_TILE if t_valid is None else 1
    NB = 3 * WB + 2 * LANES
    lb = (cb + 3) * (WB // LANES)
    s0 = wkv.reshape(B, npair, 2, HD_B, HD_B).transpose(0, 1, 3, 2, 4).reshape(B, npair * HD_B, LANES)
    zpad = jnp.zeros((HD_B, WB), F32)
    w2 = jnp.concatenate([P['rwkv_w2'], zpad], axis=0)
    a2 = jnp.concatenate([zpad, P['rwkv_a2']], axis=0)
    hi = lambda w: w.astype(BF16)
    lo = lambda w: (w - w.astype(BF16).astype(F32)).astype(BF16)
    row = lambda x: x.reshape(1, -1)
    full = lambda shape: pl.BlockSpec(shape, lambda b, j: (0,) * len(shape))
    tok = lambda w, c: pl.BlockSpec((G, RW_TILE, w), lambda b, j, c=c: (b, j, c))
    SR = npair * HD_B
    y, snew = pl.pallas_call(
        functools.partial(_rwkv_kernel, G=G, nt=nt, t_valid=t_valid),
        grid=(B // G, nt),
        in_specs=[tok(WB, cb), tok(WB, cb + 1), tok(WB, cb + 2), tok(LANES, lb), tok(LANES, lb + 1),
                  pl.BlockSpec((G, 1, NB), lambda b, j: (b, 0, 0)),
                  pl.BlockSpec((G, SR, LANES), lambda b, j: (b, 0, 0)),
                  full((1, NB)), full((1, WB)), full((LANES, WB)), full((LANES, WB)), full((1, WB)),
                  full((LANES, WB)), full((LANES, WB)), full((LANES, WB)), full((1, WB)), full((1, WB)),
                  full((1, WB)), full((1, WB)), full((1, WB)), full((LANES, RW_SUB * LANES))],
        out_specs=[pl.BlockSpec((G, RW_TILE, WB), lambda b, j: (b, j, 0)),
                   pl.BlockSpec((G, SR, LANES), lambda b, j: (b, 0, 0))],
        out_shape=[jax.ShapeDtypeStruct((B, nt * RW_TILE, WB), BF16),
                   jax.ShapeDtypeStruct((B, SR, LANES), F32)],
        scratch_shapes=[pltpu.VMEM((G, SR, LANES), F32),
                        pltpu.VMEM((G, 1, NB), F32)]
                       + [pltpu.VMEM((G, RW_TILE, WB), F32)] * 9
                       + [pltpu.VMEM((G, SR, 2 * RW_TILE), F32)] * 2
                       + [pltpu.VMEM((G, RW_SUB, SR, LANES), F32),
                          pltpu.VMEM((G, RW_TILE // HD_B, SR, LANES), F32)],
        compiler_params=_cp(("parallel", "arbitrary")),
        name="rwkv7",
    )(u3, u3, u3, u3, u3, shift.reshape(B, 1, NB), s0, row(P['rwkv_mu']), row(P['rwkv_w0']), hi(w2), lo(w2),
      row(P['rwkv_a0']), hi(a2), lo(a2), hi(P['rwkv_g2']), row(P['rwkv_kk']), row(P['rwkv_ka']),
      row(P['rwkv_rk']), row(P['rwkv_gn_g']), row(P['rwkv_gn_b']), _rwkv_sel())
    snew = snew.reshape(B, npair, HD_B, 2, HD_B).transpose(0, 1, 3, 2, 4).reshape(B, H, HD_B, HD_B)
    return y, snew


def _fox_prep_kernel(f_ref, bf_ref, logf_ref, cq_ref, ck_ref, carry_ref, *, tc):
    j = pl.program_id(1)

    @pl.when(j == 0)
    def _():
        carry_ref[...] = jnp.zeros_like(carry_ref)

    lf = _log_sigmoid(f_ref[0] + bf_ref[...])
    logf_ref[0] = lf
    tri = (_iota2((tc, tc), 0) >= _iota2((tc, tc), 1)).astype(F32)
    c = jnp.dot(tri, lf, precision=HI, preferred_element_type=F32) + carry_ref[...]
    carry_ref[...] = c[tc - 1:tc, :]
    ct = c.T
    for h in range(H_C):
        cq_ref[0, h] = jnp.broadcast_to(c[:, h:h + 1], (tc, LANES))
        ck_ref[0, h] = ct[h:h + 1, :]


def _fox_prep(u3, fcol, bf_pad):
    B, T, _ = u3.shape
    tc = _pick(T, (256,))
    return pl.pallas_call(
        functools.partial(_fox_prep_kernel, tc=tc),
        grid=(B, T // tc),
        in_specs=[pl.BlockSpec((1, tc, LANES), lambda b, j: (b, j, fcol)),
                  pl.BlockSpec((1, LANES), lambda b, j: (0, 0))],
        out_specs=[pl.BlockSpec((1, tc, LANES), lambda b, j: (b, j, 0)),
                   pl.BlockSpec((1, H_C, tc, LANES), lambda b, j: (b, 0, j, 0)),
                   pl.BlockSpec((1, H_C, 1, tc), lambda b, j: (b, 0, 0, j))],
        out_shape=[jax.ShapeDtypeStruct((B, T, LANES), F32),
                   jax.ShapeDtypeStruct((B, H_C, T, LANES), F32),
                   jax.ShapeDtypeStruct((B, H_C, 1, T), F32)],
        scratch_shapes=[pltpu.VMEM((1, LANES), F32)],
        compiler_params=_cp(("parallel", "arbitrary")),
        name="fox_prep",
    )(u3, bf_pad)


def _fox_flash_kernel(q_ref, k_ref, v_ref, cq_ref, ck_ref, o_ref, m_ref, l_ref, acc_ref,
                      *, tq, tk, nk, scale):
    i = pl.program_id(2)
    j = pl.program_id(3)

    @pl.when(j == 0)
    def _():
        m_ref[...] = jnp.full_like(m_ref, -jnp.inf)
        l_ref[...] = jnp.zeros_like(l_ref)
        acc_ref[...] = jnp.zeros_like(acc_ref)

    @pl.when(j * tk <= i * tq + tq - 1)
    def _():
        s = lax.dot_general(q_ref[0].astype(BF16), k_ref[0].astype(BF16), (((1,), (1,)), ((), ())),
                            preferred_element_type=F32) * scale
        rep = tk // LANES
        s = s + jnp.concatenate([cq_ref[0, 0]] * rep, axis=1) - ck_ref[0, 0]
        qpos = i * tq + _iota2((tq, tk), 0)
        kpos = j * tk + _iota2((tq, tk), 1)
        s = jnp.where(kpos <= qpos, s, -jnp.inf)
        m_prev = m_ref[...]
        m_new = jnp.maximum(m_prev, jnp.max(s, axis=1, keepdims=True))
        p = jnp.exp(s - jnp.concatenate([m_new] * rep, axis=1))
        alpha = jnp.exp(m_prev - m_new)
        l_ref[...] = alpha * l_ref[...] + jnp.sum(p, axis=1, keepdims=True)
        acc_ref[...] = alpha * acc_ref[...] + jnp.dot(p.astype(BF16), v_ref[0].astype(BF16),
                                                      preferred_element_type=F32)
        m_ref[...] = m_new

    @pl.when(j == nk - 1)
    def _():
        o_ref[0] = (acc_ref[...] / l_ref[...]).astype(o_ref.dtype)


def _fox_prompt(u3, qcol, cq, ck, hd):
    B, T, _ = u3.shape
    tq = tk = _pick(T, (512, 256))
    nq, nk = T // tq, T // tk
    kmap = lambda off: (lambda b, h, i, j: (b, jnp.minimum(j, (i * tq + tq - 1) // tk), off + h))
    return pl.pallas_call(
        functools.partial(_fox_flash_kernel, tq=tq, tk=tk, nk=nk, scale=hd ** -0.5),
        grid=(B, H_C, nq, nk),
        in_specs=[pl.BlockSpec((1, tq, hd), lambda b, h, i, j: (b, i, qcol + h)),
                  pl.BlockSpec((1, tk, hd), kmap(qcol + H_C)),
                  pl.BlockSpec((1, tk, hd), kmap(qcol + 2 * H_C)),
                  pl.BlockSpec((1, 1, tq, LANES), lambda b, h, i, j: (b, h, i, 0)),
                  pl.BlockSpec((1, 1, 1, tk),
                               lambda b, h, i, j: (b, h, 0, jnp.minimum(j, (i * tq + tq - 1) // tk)))],
        out_specs=pl.BlockSpec((1, tq, hd), lambda b, h, i, j: (b, i, h)),
        out_shape=jax.ShapeDtypeStruct((B, T, H_C * hd), BF16),
        scratch_shapes=[pltpu.VMEM((tq, LANES), F32), pltpu.VMEM((tq, LANES), F32),
                        pltpu.VMEM((tq, hd), F32)],
        compiler_params=_cp(("parallel", "parallel", "parallel", "arbitrary")),
        name="fox_prompt",
    )(u3, u3, u3, cq, ck)


def _fox_decay_kernel(pt_ref, *refs, nb):
    clf_refs, d_ref, tail_ref = refs[:nb], refs[nb], refs[nb + 1]
    j = pl.program_id(0)

    @pl.when(j == 0)
    def _():
        tail_ref[...] = jnp.zeros_like(tail_ref)

    lf = jnp.concatenate([r[0, 0] for r in clf_refs], axis=0)
    later = (_iota2((PAGE, PAGE), 0) > _iota2((PAGE, PAGE), 1)).astype(F32)
    d = jnp.dot(lf, later, precision=HI, preferred_element_type=F32) + tail_ref[...]
    tail_ref[...] = tail_ref[...] + jnp.dot(lf, jnp.ones((PAGE, PAGE), F32), precision=HI,
                                            preferred_element_type=F32)
    for b in range(nb):
        d_ref[b, 0] = d[H_C * b:H_C * (b + 1)]


def _fox_decay(clf_t, layer, page_table):
    B, npg = page_table.shape
    page = lambda b: (lambda j, pt: (layer, pt[b, npg - 1 - j], 0, 0))
    grid_spec = pltpu.PrefetchScalarGridSpec(
        num_scalar_prefetch=1,
        grid=(npg,),
        in_specs=[pl.BlockSpec((1, 1, H_C, PAGE), page(b)) for b in range(B)],
        out_specs=pl.BlockSpec((B, 1, H_C, PAGE), lambda j, pt: (0, npg - 1 - j, 0, 0)),
        scratch_shapes=[pltpu.VMEM((B * H_C, PAGE), F32)])
    return pl.pallas_call(
        functools.partial(_fox_decay_kernel, nb=B),
        grid_spec=grid_spec,
        out_shape=jax.ShapeDtypeStruct((B, npg, H_C, PAGE), F32),
        compiler_params=_cp(("arbitrary",)),
        name="fox_decay",
    )(page_table, *([clf_t] * B))


def _fox_sample_kernel(pt_ref, q_ref, kn_ref, vn_ref, f_ref, bf_ref, dp_ref, *refs, G, tn, nsteps, scale):
    k_refs, v_refs = refs[:G], refs[G:2 * G]
    o_ref, lfo_ref, m_ref, l_ref, acc_ref, cnq_ref = refs[2 * G:]
    j = pl.program_id(1)
    R = tn * H_C
    hd = q_ref.shape[-1]
    PW = PAGE * H_C
    rep = lambda x, n: jnp.concatenate([x] * n, axis=1)

    @pl.when(j == 0)
    def _():
        lane = _iota2((H_C, LANES), 1)
        lf = jnp.where(lane < R, _log_sigmoid(jnp.broadcast_to(f_ref[0], (H_C, LANES)) + bf_ref[...]), 0.0)
        lfo_ref[0] = lf[0:1]
        r_, c_ = _iota2((LANES, LANES), 0), _iota2((LANES, LANES), 1)
        mcum = ((r_ % H_C == c_ % H_C) & (r_ // H_C <= c_ // H_C)).astype(F32)
        c_row = jnp.dot(lf, mcum, precision=HI, preferred_element_type=F32)[0:1]
        diag = jnp.where(_iota2((R, LANES), 0) == _iota2((R, LANES), 1),
                         jnp.broadcast_to(c_row, (R, LANES)), 0.0)
        cnq = jnp.dot(diag, jnp.ones((LANES, LANES), F32), precision=HI, preferred_element_type=F32)
        cnq_ref[...] = cnq
        zpad = jnp.zeros((LANES - R, hd), F32)
        kn = jnp.concatenate([kn_ref[0], zpad], axis=0).astype(BF16)
        vn = jnp.concatenate([vn_ref[0], zpad], axis=0).astype(BF16)
        s = lax.dot_general(q_ref[0].astype(BF16), kn, (((1,), (1,)), ((), ())),
                            preferred_element_type=F32) * scale
        s = s + cnq - c_row
        row, col = _iota2((R, LANES), 0), _iota2((R, LANES), 1)
        ok = (row % H_C == col % H_C) & (col // H_C <= row // H_C)
        s = jnp.where(ok, s, -jnp.inf)
        m_new = jnp.max(s, axis=1, keepdims=True)
        p = jnp.exp(s - m_new)
        l_ref[...] = jnp.broadcast_to(jnp.sum(p, axis=1, keepdims=True), (R, LANES))
        acc_ref[...] = jnp.dot(p.astype(BF16), vn, preferred_element_type=F32)
        m_ref[...] = jnp.broadcast_to(m_new, (R, LANES))

    q = q_ref[0].astype(BF16)
    own = (_iota2((R, PW), 0) % H_C) == (_iota2((R, PW), 1) % H_C)
    cnq_w = rep(cnq_ref[...], PW // LANES)
    ss = []
    for g in range(G):
        s = lax.dot_general(q, k_refs[g][0, 0].astype(BF16), (((1,), (1,)), ((), ())),
                            preferred_element_type=F32) * scale
        ss.append(jnp.where(own, s + cnq_w + dp_ref[0, g], -jnp.inf))
    m_prev = m_ref[...]
    m_cur = jnp.max(ss[0], axis=1, keepdims=True)
    for g in range(1, G):
        m_cur = jnp.maximum(m_cur, jnp.max(ss[g], axis=1, keepdims=True))
    m_new = jnp.maximum(m_prev, m_cur)
    m_w = rep(m_new, PW // LANES)
    lsum, pv = None, None
    for g in range(G):
        p = jnp.exp(ss[g] - m_w)
        ls = jnp.sum(p, axis=1, keepdims=True)
        d = jnp.dot(p.astype(BF16), v_refs[g][0, 0].astype(BF16), preferred_element_type=F32)
        lsum = ls if lsum is None else lsum + ls
        pv = d if pv is None else pv + d
    alpha = jnp.exp(m_prev - m_new)
    l_ref[...] = alpha * l_ref[...] + lsum
    acc_ref[...] = alpha * acc_ref[...] + pv
    m_ref[...] = m_new

    @pl.when(j == nsteps - 1)
    def _():
        o_ref[0] = (acc_ref[...] / l_ref[...]).astype(o_ref.dtype)


def _fox_sample(qm, kn, vn, f_row, bf_row, dp, ck4, cv4, layer, page_table):
    B, R, hd = qm.shape
    tn = R // H_C
    npg = page_table.shape[1]
    G = _pick(npg, (8, 4, 2, 1))
    nsteps = npg // G
    PW = PAGE * H_C
    seq = lambda b, j, pt: (b, 0, 0)
    page = lambda g: (lambda b, j, pt: (layer, pt[b, G * j + g], 0, 0))
    grid_spec = pltpu.PrefetchScalarGridSpec(
        num_scalar_prefetch=1,
        grid=(B, nsteps),
        in_specs=[pl.BlockSpec((1, R, hd), seq), pl.BlockSpec((1, R, hd), seq), pl.BlockSpec((1, R, hd), seq),
                  pl.BlockSpec((1, 1, LANES), seq),
                  pl.BlockSpec((1, LANES), lambda b, j, pt: (0, 0)),
                  pl.BlockSpec((1, G, 1, PW), lambda b, j, pt: (b, j, 0, 0))]
                 + [pl.BlockSpec((1, 1, PW, hd), page(g)) for g in range(G)] * 2,
        out_specs=[pl.BlockSpec((1, R, hd), seq), pl.BlockSpec((1, 1, LANES), seq)],
        scratch_shapes=[pltpu.VMEM((R, LANES), F32), pltpu.VMEM((R, LANES), F32),
                        pltpu.VMEM((R, hd), F32), pltpu.VMEM((R, LANES), F32)])
    return pl.pallas_call(
        functools.partial(_fox_sample_kernel, G=G, tn=tn, nsteps=nsteps, scale=hd ** -0.5),
        grid_spec=grid_spec,
        out_shape=[jax.ShapeDtypeStruct((B, R, hd), BF16), jax.ShapeDtypeStruct((B, 1, LANES), F32)],
        compiler_params=_cp(("parallel", "arbitrary")),
        name="fox_sample",
    )(page_table, qm, kn, vn, f_row, bf_row, dp, *([ck4] * G), *([cv4] * G))


def _gla_kernel(q_ref, k_ref, v_ref, g_ref, ld_ref, aw_ref, ab_ref, ng_ref, s0_ref, y_ref, sout_ref,
                s_ref, o_ref, *, tt, nt, t_valid):
    j = pl.program_id(2)
    dk = q_ref.shape[-1]
    C = GLA_CHUNK
    nch = tt // C

    @pl.when(j == 0)
    def _():
        s_ref[...] = s0_ref[0, 0]

    z = jnp.dot(ld_ref[0], aw_ref[...], precision=HI, preferred_element_type=F32) + ab_ref[...]
    lga = _log_sigmoid(z) * (1.0 / GLA_TAU)
    kx = k_ref[0]
    if t_valid is not None:
        valid = _iota2((tt, dk), 0) < t_valid
        lga = jnp.where(valid, lga, 0.0)
        kx = jnp.where(valid, kx, 0.0)
    same = (_iota2((tt, tt), 0) // C) == (_iota2((tt, tt), 1) // C)
    upto = _iota2((tt, tt), 0) >= _iota2((tt, tt), 1)
    b = jnp.dot((same & upto).astype(F32), lga, precision=HI, preferred_element_type=F32)
    bend = jnp.dot(same.astype(F32), lga, precision=HI, preferred_element_type=F32)
    qx = q_ref[0] * (dk ** -0.5)
    qe = qx * jnp.exp(b)
    kdt = (kx * jnp.exp(bend - b)).T
    kdt_h = kdt.astype(BF16)
    kdt_l = (kdt - kdt_h.astype(F32)).astype(BF16)
    dect = jnp.exp(bend).T
    v = v_ref[0]
    v_h = v.astype(BF16)
    v_l = (v - v_h.astype(F32)).astype(BF16)
    zero = jnp.zeros_like(v_h)
    rowc = _iota2((tt, v.shape[-1]), 0) // C
    tri_rows = _iota2((C, dk), 0)
    for c in range(nch):
        r0 = C * c
        q_c, k_c, b_c, v_c = qx[r0:r0 + C], kx[r0:r0 + C], b[r0:r0 + C], v[r0:r0 + C]
        o_c = jnp.dot(qe[r0:r0 + C].astype(BF16), s_ref[...].astype(BF16), preferred_element_type=F32)
        for s in range(C):
            rel = jnp.exp(jnp.where(tri_rows >= s, b_c - b_c[s:s + 1, :], -jnp.inf))
            att = jnp.sum(q_c * k_c[s:s + 1, :] * rel, axis=1, keepdims=True)
            o_c = o_c + att * v_c[s:s + 1, :]
        o_ref[r0:r0 + C, :] = o_c
        vm_h = jnp.where(rowc == c, v_h, zero)
        vm_l = jnp.where(rowc == c, v_l, zero)
        s_ref[...] = (s_ref[...] * dect[:, r0:r0 + 1]
                      + jnp.dot(kdt_h, vm_h, preferred_element_type=F32)
                      + jnp.dot(kdt_l, vm_h, preferred_element_type=F32)
                      + jnp.dot(kdt_h, vm_l, preferred_element_type=F32))
    od = o_ref[...]
    od = od * lax.rsqrt(jnp.mean(od * od, axis=-1, keepdims=True) + LN_EPS)
    gate = g_ref[0]
    y_ref[0] = (od * ng_ref[...] * (gate * jax.nn.sigmoid(gate))).astype(y_ref.dtype)

    @pl.when(j == nt - 1)
    def _():
        sout_ref[0, 0] = s_ref[...]


def _gla_mixer(u3, qc, kc, vc, gc, ldc, aw_pad, ab, ng, s0, t_valid=None):
    B, Tp, _ = u3.shape
    _, H, dk, dv = s0.shape
    tt = _pick(Tp, (256, 128))
    nt = Tp // tt if t_valid is None else 1
    return pl.pallas_call(
        functools.partial(_gla_kernel, tt=tt, nt=nt, t_valid=t_valid),
        grid=(B, H, nt),
        in_specs=[pl.BlockSpec((1, tt, dk), lambda b, h, j: (b, j, qc + h)),
                  pl.BlockSpec((1, tt, dk), lambda b, h, j: (b, j, kc + h)),
                  pl.BlockSpec((1, tt, dv), lambda b, h, j: (b, j, vc + h)),
                  pl.BlockSpec((1, tt, dv), lambda b, h, j: (b, j, gc + h)),
                  pl.BlockSpec((1, tt, LANES), lambda b, h, j: (b, j, ldc)),
                  pl.BlockSpec((LANES, dk), lambda b, h, j: (0, h)),
                  pl.BlockSpec((1, dk), lambda b, h, j: (0, h)),
                  pl.BlockSpec((1, dv), lambda b, h, j: (0, h)),
                  pl.BlockSpec((1, 1, dk, dv), lambda b, h, j: (b, h, 0, 0))],
        out_specs=[pl.BlockSpec((1, tt, dv), lambda b, h, j: (b, j, h)),
                   pl.BlockSpec((1, 1, dk, dv), lambda b, h, j: (b, h, 0, 0))],
        out_shape=[jax.ShapeDtypeStruct((B, nt * tt, H * dv), BF16),
                   jax.ShapeDtypeStruct((B, H, dk, dv), F32)],
        scratch_shapes=[pltpu.VMEM((dk, dv), F32), pltpu.VMEM((tt, dv), F32)],
        compiler_params=_cp(("parallel", "parallel", "arbitrary")),
        name="gla",
    )(u3, u3, u3, u3, u3, aw_pad, ab.reshape(1, -1), ng.reshape(1, -1), s0)


def _pack_w_in(w, wa, wb, wc, wd, dkq):
    D = w.shape[0]
    n_a, n_b = 2 * wa, 3 * wb + 2 * LANES
    c0 = n_a + n_b
    fc = c0 + 3 * wc
    d0 = fc + H_C
    ld0 = d0 + 2 * dkq + 2 * wd
    n_ld = w.shape[1] - ld0
    return jnp.concatenate([w[:, :fc], w[:, d0:ld0], w[:, fc:d0], jnp.zeros((D, LANES - H_C), w.dtype),
                            w[:, ld0:], jnp.zeros((D, LANES - n_ld), w.dtype)], axis=1)


def _token_mixing(xb, B, T, P, layer, st, sample):
    M, D = xb.shape
    wa = wb = wc = D // 4
    wd = D - 3 * (D // 4)
    dkq = wd // 2
    hd = wc // H_C
    n_ld = P['gla_aw'].shape[0]
    w_in = _pack_w_in(P['w_in'], wa, wb, wc, wd, dkq)
    N = w_in.shape[1]
    tm = _pick(M, (1024, 32))
    u = _matmul(xb, w_in, (), tm, _pick(N, (512,)), D)
    u3 = u.reshape(B, T, N)
    cB = 2 * wa
    cC = cB + 3 * wb + 2 * LANES
    cD = cC + 3 * wc
    cF = cD + 2 * dkq + 2 * wd
    cL = cF + LANES
    bf_pad = jnp.concatenate([P['fox_bf'], jnp.zeros((LANES - H_C,), F32)])
    aw_pad = jnp.concatenate([P['gla_aw'], jnp.zeros((LANES - n_ld, dkq), F32)], axis=0)
    kc = u3[:, :, cC + wc:cC + 2 * wc].reshape(B, T, H_C, hd)
    vc = u3[:, :, cC + 2 * wc:cC + 3 * wc].reshape(B, T, H_C, hd)
    shift_new = u3[:, T - 1, cB:cB + 3 * wb + 2 * LANES]

    ya, conv_new = _conv_mixer(u3, st['conv'], P['conv_w'], P['conv_b'], P['conv_ln_g'], P['conv_ln_b'])

    if sample is None:
        up, t_valid = u3, None
    else:
        up, t_valid = jnp.pad(u3, ((0, 0), (0, RW_TILE - T), (0, 0))), T
    yb, wkv_new = _rwkv_mixer(up, cB // wb, st['shift'], st['wkv'], P, t_valid)
    yd, gla_new = _gla_mixer(up, cD // (dkq // H_D), (cD + dkq) // (dkq // H_D),
                             (cD + 2 * dkq) // (wd // H_D), (cD + 2 * dkq + wd) // (wd // H_D),
                             cL // LANES, aw_pad, P['gla_ab'], P['gla_ng'], st['gla'], t_valid)
    yb, yd = yb[:, :T], yd[:, :T]

    if sample is None:
        logf_pad, cq, ck = _fox_prep(u3, cF // LANES, bf_pad.reshape(1, LANES))
        logf = logf_pad[:, :, :H_C]
        yc = _fox_prompt(u3, cC // hd, cq, ck, hd)
    else:
        R = T * H_C
        rows = lambda c0: u3[:, :, c0:c0 + wc].reshape(B, R, hd)
        f_row = jnp.pad(u3[:, :, cF:cF + H_C].reshape(B, 1, R), ((0, 0), (0, 0), (0, LANES - R)))
        bf_row = jnp.pad(jnp.tile(P['fox_bf'], T), (0, LANES - R)).reshape(1, LANES)
        dpt = _fox_decay(sample['clf_t'], layer, sample['page_table'])
        dp = dpt.transpose(0, 1, 3, 2).reshape(B, dpt.shape[1], 1, PAGE * H_C)
        yc, lfo = _fox_sample(rows(cC), rows(cC + wc), rows(cC + 2 * wc), f_row, bf_row, dp,
                              sample['ck'], sample['cv'], layer, sample['page_table'])
        yc = yc.reshape(B, T, wc)
        logf = lfo[:, 0, :R].reshape(B, T, H_C)

    ymix = jnp.concatenate([ya, yb, yc, yd], axis=-1).reshape(M, D)
    y = _matmul(ymix, P['w_o_all'], (layer,), tm, _pick(D, (512,)), D)
    return y, (kc, vc, logf, conv_new, shift_new, wkv_new, gla_new)


def _swiglu(xb, P, layer, which):
    M, D = xb.shape
    F = P['ffn_wg_all'].shape[-1]
    widx = (layer, which)
    h = _ffn_up(xb, P['ffn_wg_all'], P['ffn_wu_all'], widx, _pick(M, (1024, 32)), _pick(F, (256,)))
    return _matmul(h, P['ffn_wd_all'], widx, _pick(M, (512, 32)), _pick(D, (256,)), F)


def _layer(x, xb, B, T, P, layer, st, sample, alpha):
    x, xb = _add_ln(x, _swiglu(xb, P, layer, 0), P['ln_g'][0], P['ln_b'][0], alpha, 0.5)
    y, new = _token_mixing(xb, B, T, P, layer, st, sample)
    x, xb = _add_ln(x, y, P['ln_g'][1], P['ln_b'][1], alpha, 1.0)
    x, xb = _add_ln(x, _swiglu(xb, P, layer, 1), P['ln_g'][2], P['ln_b'][2], alpha, 0.5)
    return x, xb, new


def kernel(x_prompt, x_sample, cache_k, cache_v, cache_logf, state_conv, state_shift, state_wkv, state_gla, page_table, ln_g, ln_b, ffn_wg, ffn_wu, ffn_wd, w_in, w_o, conv_w, conv_b, conv_ln_g, conv_ln_b, rwkv_mu, rwkv_w0, rwkv_w2, rwkv_a0, rwkv_a2, rwkv_g2, rwkv_kk, rwkv_ka, rwkv_rk, rwkv_gn_g, rwkv_gn_b, fox_bf, gla_aw, gla_ab, gla_ng):
    big = dict(ffn_wg_all=ffn_wg, ffn_wu_all=ffn_wu, ffn_wd_all=ffn_wd, w_o_all=w_o)
    stacked = dict(ln_g=ln_g, ln_b=ln_b, w_in=w_in, conv_w=conv_w, conv_b=conv_b, conv_ln_g=conv_ln_g,
                   conv_ln_b=conv_ln_b, rwkv_mu=rwkv_mu, rwkv_w0=rwkv_w0, rwkv_w2=rwkv_w2, rwkv_a0=rwkv_a0,
                   rwkv_a2=rwkv_a2, rwkv_g2=rwkv_g2, rwkv_kk=rwkv_kk, rwkv_ka=rwkv_ka, rwkv_rk=rwkv_rk,
                   rwkv_gn_g=rwkv_gn_g, rwkv_gn_b=rwkv_gn_b, fox_bf=fox_bf, gla_aw=gla_aw, gla_ab=gla_ab,
                   gla_ng=gla_ng)
    depth = w_in.shape[0]
    Bp, Tp, D = x_prompt.shape
    Bs, Ts, _ = x_sample.shape
    alpha = (2.0 * depth) ** 0.25
    n_pool, page, hc, hdc = cache_k.shape[1:]
    ck = cache_k.reshape(depth, n_pool, page * hc, hdc)
    cv = cache_v.reshape(depth, n_pool, page * hc, hdc)
    clf_t = cache_logf.transpose(0, 1, 3, 2)
    xp, xs = x_prompt.reshape(Bp * Tp, D), x_sample.reshape(Bs * Ts, D)
    xpb, xsb = xp.astype(BF16), xs.astype(BF16)
    zero_state = dict(conv=jnp.zeros((Bp,) + state_conv.shape[2:], F32),
                      shift=jnp.zeros((Bp,) + state_shift.shape[2:], F32),
                      wkv=jnp.zeros((Bp,) + state_wkv.shape[2:], F32),
                      gla=jnp.zeros((Bp,) + state_gla.shape[2:], F32))
    st_p, st_s = [], []
    for l in range(depth):
        P = {n: a[l] for n, a in stacked.items()}
        P.update(big)
        xp, xpb, s = _layer(xp, xpb, Bp, Tp, P, l, zero_state, None, alpha)
        st_p.append(s)
        st = dict(conv=state_conv[l], shift=state_shift[l], wkv=state_wkv[l], gla=state_gla[l])
        xs, xsb, s = _layer(xs, xsb, Bs, Ts, P, l, st,
                            dict(ck=ck, cv=cv, clf_t=clf_t, page_table=page_table), alpha)
        st_s.append(s)
    stk = lambda sts, i: jnp.stack([s[i] for s in sts])
    return ((xp.reshape(Bp, Tp, D), xs.reshape(Bs, Ts, D))
            + tuple(stk(st_p, i) for i in range(7)) + tuple(stk(st_s, i) for i in range(7)))
```

```python
import functools

import numpy as np
import jax
import jax.numpy as jnp
from jax import lax
from jax.experimental import pallas as pl
from jax.experimental.pallas import tpu as pltpu

F32 = jnp.float32
BF16 = jnp.bfloat16
HI = lax.Precision.HIGHEST

LANES = 128
VMEM_LIMIT = 56 * 1024 * 1024

LN_EPS = 1e-5
GN_EPS = 64e-5
CONV_K = 31
HD_B = 64
H_C = 8
H_D = 4
GLA_TAU = 16.0
GLA_CHUNK = 16
PAGE = 128
RW_TILE = 128
RW_SUB = 16


def _cp(sem):
    return pltpu.CompilerParams(dimension_semantics=sem, vmem_limit_bytes=VMEM_LIMIT)


def _pick(n, prefs):
    for p in prefs:
        if n % p == 0:
            return p
    return n


def _log_sigmoid(x):
    return jnp.minimum(x, 0.0) - jnp.log1p(jnp.exp(-jnp.abs(x)))


def _softplus(x):
    return jnp.maximum(x, 0.0) + jnp.log1p(jnp.exp(-jnp.abs(x)))


def _split_dot(x, w_bf16, n):
    acc = None
    rem = x
    for i in range(n):
        h = rem.astype(BF16)
        d = jnp.dot(h, w_bf16, preferred_element_type=F32)
        acc = d if acc is None else acc + d
        if i < n - 1:
            rem = rem - h.astype(F32)
    return acc


def _iota2(shape, dim):
    return lax.broadcasted_iota(jnp.int32, shape, dim)


def _mm_kernel(x_ref, w_ref, o_ref, acc_ref, *, nk):
    part = jnp.dot(x_ref[...].astype(BF16), w_ref[...].astype(BF16), preferred_element_type=F32)
    if nk == 1:
        o_ref[...] = part.astype(o_ref.dtype)
        return
    k = pl.program_id(2)

    @pl.when(k == 0)
    def _():
        acc_ref[...] = part

    @pl.when(k > 0)
    def _():
        acc_ref[...] += part

    @pl.when(k == nk - 1)
    def _():
        o_ref[...] = acc_ref[...].astype(o_ref.dtype)


def _matmul(x, w, widx, tm, tn, tk, out_dtype=F32):
    M, K = x.shape
    N = w.shape[-1]
    nk = K // tk
    return pl.pallas_call(
        functools.partial(_mm_kernel, nk=nk),
        grid=(M // tm, N // tn, nk),
        in_specs=[pl.BlockSpec((tm, tk), lambda i, j, k: (i, k)),
                  pl.BlockSpec((None,) * len(widx) + (tk, tn), lambda i, j, k: widx + (k, j))],
        out_specs=pl.BlockSpec((tm, tn), lambda i, j, k: (i, j)),
        out_shape=jax.ShapeDtypeStruct((M, N), out_dtype),
        scratch_shapes=[pltpu.VMEM((tm, tn) if nk > 1 else (8, LANES), F32)],
        compiler_params=_cp(("parallel", "parallel", "arbitrary")),
        name="matmul",
    )(x, w)


def _mm2_kernel(x_ref, wa_ref, wb_ref, o_ref, *, na):
    j = pl.program_id(1)

    @pl.when(j < na)
    def _():
        o_ref[...] = jnp.dot(x_ref[...], wa_ref[...].astype(BF16), preferred_element_type=F32)

    @pl.when(j >= na)
    def _():
        o_ref[...] = jnp.dot(x_ref[...], wb_ref[...].astype(BF16), preferred_element_type=F32)


def _matmul2(x, wa, widx, na_cols, wb, tm, tn):
    M, K = x.shape
    na = na_cols // tn
    nb = wb.shape[1] // tn
    return pl.pallas_call(
        functools.partial(_mm2_kernel, na=na),
        grid=(M // tm, na + nb),
        in_specs=[pl.BlockSpec((tm, K), lambda i, j: (i, 0)),
                  pl.BlockSpec((None,) * len(widx) + (K, tn), lambda i, j: widx + (0, jnp.minimum(j, na - 1))),
                  pl.BlockSpec((K, tn), lambda i, j: (0, jnp.maximum(j - na, 0)))],
        out_specs=pl.BlockSpec((tm, tn), lambda i, j: (i, j)),
        out_shape=jax.ShapeDtypeStruct((M, (na + nb) * tn), F32),
        compiler_params=_cp(("parallel", "arbitrary")),
        name="matmul2",
    )(x, wa, wb)


def _ffn_up_kernel(x_ref, wg_ref, wu_ref, o_ref):
    x = x_ref[...]
    g = jnp.dot(x, wg_ref[...].astype(BF16), preferred_element_type=F32)
    u = jnp.dot(x, wu_ref[...].astype(BF16), preferred_element_type=F32)
    o_ref[...] = (g * jax.nn.sigmoid(g) * u).astype(o_ref.dtype)


def _ffn_up(xb, wg, wu, widx, tm, tn):
    M, K = xb.shape
    N = wg.shape[-1]
    wspec = pl.BlockSpec((None,) * len(widx) + (K, tn), lambda i, j: widx + (0, j))
    return pl.pallas_call(
        _ffn_up_kernel,
        grid=(M // tm, N // tn),
        in_specs=[pl.BlockSpec((tm, K), lambda i, j: (i, 0)), wspec, wspec],
        out_specs=pl.BlockSpec((tm, tn), lambda i, j: (i, j)),
        out_shape=jax.ShapeDtypeStruct((M, N), BF16),
        compiler_params=_cp(("parallel", "arbitrary")),
        name="ffn_up",
    )(xb, wg, wu)


def _add_ln_kernel(x_ref, y_ref, g_ref, b_ref, of_ref, ob_ref, *, alpha, c):
    z = alpha * x_ref[...] + c * y_ref[...]
    mu = jnp.mean(z, axis=-1, keepdims=True)
    zc = z - mu
    var = jnp.mean(zc * zc, axis=-1, keepdims=True)
    o = zc * lax.rsqrt(var + LN_EPS) * g_ref[...] + b_ref[...]
    of_ref[...] = o
    ob_ref[...] = o.astype(BF16)


def _add_ln(x, y, g, b, alpha, c):
    M, D = x.shape
    tr = _pick(M, (256, 32))
    return pl.pallas_call(
        functools.partial(_add_ln_kernel, alpha=alpha, c=c),
        grid=(M // tr,),
        in_specs=[pl.BlockSpec((tr, D), lambda i: (i, 0)),
                  pl.BlockSpec((tr, D), lambda i: (i, 0)),
                  pl.BlockSpec((1, D), lambda i: (0, 0)),
                  pl.BlockSpec((1, D), lambda i: (0, 0))],
        out_specs=[pl.BlockSpec((tr, D), lambda i: (i, 0)),
                   pl.BlockSpec((tr, D), lambda i: (i, 0))],
        out_shape=[jax.ShapeDtypeStruct((M, D), F32), jax.ShapeDtypeStruct((M, D), BF16)],
        compiler_params=_cp(("parallel",)),
        name="add_ln",
    )(x, y, g.reshape(1, D), b.reshape(1, D))


def _conv_kernel(val_ref, gate_ref, buf_ref, w_ref, cb_ref, g_ref, b_ref, y_ref, new_ref, zz_ref,
                 *, tt, nt):
    j = pl.program_id(1)
    hist = CONV_K - 1

    @pl.when(j == 0)
    def _():
        zz_ref[2:2 + hist, :] = buf_ref[0]

    zz_ref[32:32 + tt, :] = val_ref[0] * jax.nn.sigmoid(gate_ref[0])
    rc = min(tt, 32)
    for r0 in range(0, tt, rc):
        acc = None
        for k in range(CONV_K):
            term = w_ref[k:k + 1, :] * zz_ref[2 + k + r0:2 + k + r0 + rc, :]
            acc = term if acc is None else acc + term
        y = acc + cb_ref[...]
        mu = jnp.mean(y, axis=-1, keepdims=True)
        yc = y - mu
        var = jnp.mean(yc * yc, axis=-1, keepdims=True)
        yn = yc * lax.rsqrt(var + LN_EPS) * g_ref[...] + b_ref[...]
        y_ref[0, r0:r0 + rc, :] = (yn * jax.nn.sigmoid(yn)).astype(y_ref.dtype)
    tail = zz_ref[2 + tt:2 + tt + hist, :]
    zz_ref[2:2 + hist, :] = tail

    @pl.when(j == nt - 1)
    def _():
        new_ref[0] = tail


def _conv_mixer(u3, buf, cw, cb, lg, lb):
    B, T, _ = u3.shape
    W = buf.shape[-1]
    tt = _pick(T, (256,))
    nt = T // tt
    return pl.pallas_call(
        functools.partial(_conv_kernel, tt=tt, nt=nt),
        grid=(B, nt),
        in_specs=[pl.BlockSpec((1, tt, W), lambda b, j: (b, j, 0)),
                  pl.BlockSpec((1, tt, W), lambda b, j: (b, j, 1)),
                  pl.BlockSpec((1, CONV_K - 1, W), lambda b, j: (b, 0, 0)),
                  pl.BlockSpec((CONV_K, W), lambda b, j: (0, 0)),
                  pl.BlockSpec((1, W), lambda b, j: (0, 0)),
                  pl.BlockSpec((1, W), lambda b, j: (0, 0)),
                  pl.BlockSpec((1, W), lambda b, j: (0, 0))],
        out_specs=[pl.BlockSpec((1, tt, W), lambda b, j: (b, j, 0)),
                   pl.BlockSpec((1, CONV_K - 1, W), lambda b, j: (b, 0, 0))],
        out_shape=[jax.ShapeDtypeStruct((B, T, W), BF16),
                   jax.ShapeDtypeStruct((B, CONV_K - 1, W), F32)],
        scratch_shapes=[pltpu.VMEM((32 + tt, W), F32)],
        compiler_params=_cp(("parallel", "arbitrary")),
        name="conv_mixer",
    )(u3, u3, buf, cw, cb.reshape(1, W), lg.reshape(1, W), lb.reshape(1, W))


def _rwkv_kernel(r_ref, k_ref, v_ref, lwa_ref, lg_ref, shift_ref, s0_ref, mu_ref, w0_ref, w2h_ref, w2l_ref,
                 a0_ref, a2h_ref, a2l_ref, g2_ref, kkp_ref, ka_ref, rk_ref, gng_ref, gnb_ref, sel_ref,
                 y_ref, sout_ref,
                 s_ref, prev_ref, kk_s, d_s, bv_s, k2_s, rp_s, v_s, rho_s, rks_s, g_s, lhh_s, lhl_s,
                 vbuf_s, ybuf_s, *, G, nt, t_valid):
    j = pl.program_id(1)
    TT = RW_TILE
    WB = r_ref.shape[-1]
    npair = WB // LANES

    @pl.when(j == 0)
    def _():
        s_ref[...] = s0_ref[...]
        prev_ref[...] = shift_ref[...]

    ob = (_iota2((LANES, LANES), 0) // HD_B == _iota2((LANES, LANES), 1) // HD_B).astype(BF16)
    obd = (_iota2((2 * LANES, 2 * LANES), 0) // HD_B == _iota2((2 * LANES, 2 * LANES), 1) // HD_B).astype(BF16)

    def segsum(x):
        return jnp.concatenate(
            [_split_dot(x[:, LANES * p:LANES * (p + 1)], ob, 2) for p in range(npair)], axis=1)

    def dot3(x, wh_ref, wl_ref):
        xh = x.astype(BF16)
        xl = (x - xh.astype(F32)).astype(BF16)
        wh = wh_ref[...]
        return (jnp.dot(xh, wh, preferred_element_type=F32) + jnp.dot(xl, wh, preferred_element_type=F32)
                + jnp.dot(xh, wl_ref[...], preferred_element_type=F32))

    row0 = _iota2((TT, 1), 0) == 0
    last = (t_valid if t_valid is not None else TT) - 1
    for gi in range(G):
        def shifted(x, lo, hi, gi=gi):
            prev = jnp.where(row0, prev_ref[gi, :, lo:hi], pltpu.roll(x, 1, 0))
            return x + (prev - x) * mu_ref[:, lo:hi]

        raw = (r_ref[gi], k_ref[gi], v_ref[gi], lwa_ref[gi], lg_ref[gi])
        offs = (0, WB, 2 * WB, 3 * WB, 3 * WB + LANES, 3 * WB + 2 * LANES)
        r, k, v, lwa, lg = [shifted(x, offs[i], offs[i + 1]) for i, x in enumerate(raw)]
        for i, x in enumerate(raw):
            prev_ref[gi, :, offs[i]:offs[i + 1]] = x[last:last + 1]

        w = -_softplus(-(w0_ref[...] + dot3(jnp.tanh(lwa), w2h_ref, w2l_ref))) - 0.5
        d = jnp.exp(-jnp.exp(w))
        a = jax.nn.sigmoid(a0_ref[...] + dot3(lwa, a2h_ref, a2l_ref))
        g = jnp.dot(jax.nn.sigmoid(lg).astype(BF16), g2_ref[...], preferred_element_type=F32)
        kk = k * kkp_ref[...]
        kk = kk * lax.rsqrt(jnp.maximum(segsum(kk * kk), 1e-24))
        k2 = k * (1.0 + (a - 1.0) * ka_ref[...])
        bv = kk * a
        kk_s[gi] = kk
        d_s[gi] = d
        bv_s[gi] = bv
        k2_s[gi] = k2
        rp_s[gi] = d * r - kk * segsum(bv * r)
        v_s[gi] = v
        rho_s[gi] = segsum(k2 * r)
        rks_s[gi] = segsum(r * k2 * rk_ref[...])
        g_s[gi] = g

        vt = v.T
        vh = vt.astype(BF16).astype(F32)
        vl = vt - vh
        for p in range(npair):
            for h in range(2):
                src = slice(LANES * p + HD_B * h, LANES * p + HD_B * (h + 1))
                lhh_s[gi, HD_B * p:HD_B * (p + 1), TT * h:TT * (h + 1)] = vh[src, :]
                lhl_s[gi, HD_B * p:HD_B * (p + 1), TT * h:TT * (h + 1)] = vl[src, :]

    ybuf_s[...] = jnp.zeros_like(ybuf_s)
    lane_x = _iota2((npair * HD_B, LANES), 1)
    lane_c = lane_x % HD_B
    n_steps = TT if t_valid is None else t_valid
    n_sub = -(-n_steps // RW_SUB)
    for sub in range(n_sub):
        base = (TT - RW_SUB * sub) % TT
        for gi in range(G):
            parts = [lhh_s[gi, :, 0:TT], lhh_s[gi, :, TT:2 * TT], lhl_s[gi, :, 0:TT], lhl_s[gi, :, TT:2 * TT]]
            x = 0.0
            for q in range(3, -1, -1):
                sh = (base + RW_SUB * q) % TT
                part = pltpu.roll(parts[q], sh, 1) if sh else parts[q]
                x = jnp.where(lane_x < RW_SUB * (q + 1), part, x)
            xb = x.astype(BF16)
            for t2 in range(RW_SUB // 2):
                res = jnp.dot(xb, sel_ref[:, 2 * LANES * t2:2 * LANES * (t2 + 1)], preferred_element_type=F32)
                vbuf_s[gi, 2 * t2] = res[:, 0:LANES]
                vbuf_s[gi, 2 * t2 + 1] = res[:, LANES:2 * LANES]

        def step(tl, carry, sub=sub):
            t = RW_SUB * sub + tl
            half = t // HD_B
            mask = lane_c == (t % HD_B)

            def tile(ref, gi):
                row = ref[gi, pl.ds(t, 1), :]
                return jnp.concatenate(
                    [jnp.broadcast_to(row[:, LANES * p:LANES * (p + 1)], (HD_B, LANES)) for p in range(npair)],
                    axis=0)

            for gi in range(G):
                S = s_ref[gi]
                lhs = jnp.concatenate([(S * tile(kk_s, gi)).astype(BF16), (S * tile(rp_s, gi)).astype(BF16)],
                                      axis=1)
                res = jnp.dot(lhs, obd, preferred_element_type=F32)
                s_ref[gi] = (S * tile(d_s, gi) - res[:, 0:LANES] * tile(bv_s, gi)
                             + vbuf_s[gi, tl] * tile(k2_s, gi))
                ybuf_s[gi, half] = jnp.where(mask, res[:, LANES:2 * LANES], ybuf_s[gi, half])
            return carry

        lax.fori_loop(0, min(RW_SUB, n_steps - RW_SUB * sub), step, 0, unroll=2)

    for gi in range(G):
        halves = []
        for hf in range(TT // HD_B):
            yt = ybuf_s[gi, hf].T
            cols = []
            for p in range(npair):
                cols.append(yt[0:HD_B, HD_B * p:HD_B * (p + 1)])
                cols.append(yt[HD_B:2 * HD_B, HD_B * p:HD_B * (p + 1)])
            halves.append(jnp.concatenate(cols, axis=1))
        v = v_s[gi]
        y = jnp.concatenate(halves, axis=0) + v * rho_s[gi]
        m = segsum(y) * (1.0 / HD_B)
        yc = y - m
        var = segsum(yc * yc) * (1.0 / HD_B)
        yn = yc * lax.rsqrt(var + GN_EPS) * gng_ref[...] + gnb_ref[...]
        y_ref[gi] = ((yn + rks_s[gi] * v) * g_s[gi]).astype(y_ref.dtype)

    @pl.when(j == nt - 1)
    def _():
        sout_ref[...] = s_ref[...]


def _rwkv_sel():
    rows = np.arange(LANES)
    cols = np.arange(RW_SUB * LANES)
    hit = (rows[:, None] < 4 * RW_SUB) & \
          (((rows[:, None] % (2 * RW_SUB)) // RW_SUB) == ((cols[None, :] % LANES) // HD_B)) & \
          ((rows[:, None] % RW_SUB) == (cols[None, :] // LANES))
    return jnp.asarray(hit, dtype=BF16)


def _rwkv_mixer(u3, cb, shift, wkv, P, t_valid=None):
    B, Tp, _ = u3.shape
    WB = P['rwkv_w0'].shape[-1]
    H = WB // HD_B
    npair = H // 2
    G = 2 if B % 2 == 0 else 1
    nt = Tp // RW_TILE if t_valid is None else 1
    NB = 3 * WB + 2 * LANES
    lb = (cb + 3) * (WB // LANES)
    s0 = wkv.reshape(B, npair, 2, HD_B, HD_B).transpose(0, 1, 3, 2, 4).reshape(B, npair * HD_B, LANES)
    zpad = jnp.zeros((HD_B, WB), F32)
    w2 = jnp.concatenate([P['rwkv_w2'], zpad], axis=0)
    a2 = jnp.concatenate([zpad, P['rwkv_a2']], axis=0)
    hi = lambda w: w.astype(BF16)
    lo = lambda w: (w - w.astype(BF16).astype(F32)).astype(BF16)
    row = lambda x: x.reshape(1, -1)
    full = lambda shape: pl.BlockSpec(shape, lambda b, j: (0,) * len(shape))
    tok = lambda w, c: pl.BlockSpec((G, RW_TILE, w), lambda b, j, c=c: (b, j, c))
    SR = npair * HD_B
    y, snew = pl.pallas_call(
        functools.partial(_rwkv_kernel, G=G, nt=nt, t_valid=t_valid),
        grid=(B // G, nt),
        in_specs=[tok(WB, cb), tok(WB, cb + 1), tok(WB, cb + 2), tok(LANES, lb), tok(LANES, lb + 1),
                  pl.BlockSpec((G, 1, NB), lambda b, j: (b, 0, 0)),
                  pl.BlockSpec((G, SR, LANES), lambda b, j: (b, 0, 0)),
                  full((1, NB)), full((1, WB)), full((LANES, WB)), full((LANES, WB)), full((1, WB)),
                  full((LANES, WB)), full((LANES, WB)), full((LANES, WB)), full((1, WB)), full((1, WB)),
                  full((1, WB)), full((1, WB)), full((1, WB)), full((LANES, RW_SUB * LANES))],
        out_specs=[pl.BlockSpec((G, RW_TILE, WB), lambda b, j: (b, j, 0)),
                   pl.BlockSpec((G, SR, LANES), lambda b, j: (b, 0, 0))],
        out_shape=[jax.ShapeDtypeStruct((B, nt * RW_TILE, WB), BF16),
                   jax.ShapeDtypeStruct((B, SR, LANES), F32)],
        scratch_shapes=[pltpu.VMEM((G, SR, LANES), F32),
                        pltpu.VMEM((G, 1, NB), F32)]
                       + [pltpu.VMEM((G, RW_TILE, WB), F32)] * 9
                       + [pltpu.VMEM((G, SR, 2 * RW_TILE), F32)] * 2
                       + [pltpu.VMEM((G, RW_SUB, SR, LANES), F32),
                          pltpu.VMEM((G, RW_TILE // HD_B, SR, LANES), F32)],
        compiler_params=_cp(("parallel", "arbitrary")),
        name="rwkv7",
    )(u3, u3, u3, u3, u3, shift.reshape(B, 1, NB), s0, row(P['rwkv_mu']), row(P['rwkv_w0']), hi(w2), lo(w2),
      row(P['rwkv_a0']), hi(a2), lo(a2), hi(P['rwkv_g2']), row(P['rwkv_kk']), row(P['rwkv_ka']),
      row(P['rwkv_rk']), row(P['rwkv_gn_g']), row(P['rwkv_gn_b']), _rwkv_sel())
    snew = snew.reshape(B, npair, HD_B, 2, HD_B).transpose(0, 1, 3, 2, 4).reshape(B, H, HD_B, HD_B)
    return y, snew


def _fox_prep_kernel(f_ref, bf_ref, logf_ref, cq_ref, ck_ref, carry_ref, *, tc):
    j = pl.program_id(1)

    @pl.when(j == 0)
    def _():
        carry_ref[...] = jnp.zeros_like(carry_ref)

    lf = _log_sigmoid(f_ref[0] + bf_ref[...])
    logf_ref[0] = lf
    tri = (_iota2((tc, tc), 0) >= _iota2((tc, tc), 1)).astype(F32)
    c = jnp.dot(tri, lf, precision=HI, preferred_element_type=F32) + carry_ref[...]
    carry_ref[...] = c[tc - 1:tc, :]
    ct = c.T
    for h in range(H_C):
        cq_ref[0, h] = jnp.broadcast_to(c[:, h:h + 1], (tc, LANES))
        ck_ref[0, h] = ct[h:h + 1, :]


def _fox_prep(u3, fcol, bf_pad):
    B, T, _ = u3.shape
    tc = _pick(T, (256,))
    return pl.pallas_call(
        functools.partial(_fox_prep_kernel, tc=tc),
        grid=(B, T // tc),
        in_specs=[pl.BlockSpec((1, tc, LANES), lambda b, j: (b, j, fcol)),
                  pl.BlockSpec((1, LANES), lambda b, j: (0, 0))],
        out_specs=[pl.BlockSpec((1, tc, LANES), lambda b, j: (b, j, 0)),
                   pl.BlockSpec((1, H_C, tc, LANES), lambda b, j: (b, 0, j, 0)),
                   pl.BlockSpec((1, H_C, 1, tc), lambda b, j: (b, 0, 0, j))],
        out_shape=[jax.ShapeDtypeStruct((B, T, LANES), F32),
                   jax.ShapeDtypeStruct((B, H_C, T, LANES), F32),
                   jax.ShapeDtypeStruct((B, H_C, 1, T), F32)],
        scratch_shapes=[pltpu.VMEM((1, LANES), F32)],
        compiler_params=_cp(("parallel", "arbitrary")),
        name="fox_prep",
    )(u3, bf_pad)


def _fox_flash_kernel(q_ref, k_ref, v_ref, cq_ref, ck_ref, o_ref, m_ref, l_ref, acc_ref,
                      *, tq, tk, nk, scale):
    i = pl.program_id(2)
    j = pl.program_id(3)

    @pl.when(j == 0)
    def _():
        m_ref[...] = jnp.full_like(m_ref, -jnp.inf)
        l_ref[...] = jnp.zeros_like(l_ref)
        acc_ref[...] = jnp.zeros_like(acc_ref)

    @pl.when(j * tk <= i * tq + tq - 1)
    def _():
        s = lax.dot_general(q_ref[0].astype(BF16), k_ref[0].astype(BF16), (((1,), (1,)), ((), ())),
                            preferred_element_type=F32) * scale
        rep = tk // LANES
        s = s + jnp.concatenate([cq_ref[0, 0]] * rep, axis=1) - ck_ref[0, 0]
        qpos = i * tq + _iota2((tq, tk), 0)
        kpos = j * tk + _iota2((tq, tk), 1)
        s = jnp.where(kpos <= qpos, s, -jnp.inf)
        m_prev = m_ref[...]
        m_new = jnp.maximum(m_prev, jnp.max(s, axis=1, keepdims=True))
        p = jnp.exp(s - jnp.concatenate([m_new] * rep, axis=1))
        alpha = jnp.exp(m_prev - m_new)
        l_ref[...] = alpha * l_ref[...] + jnp.sum(p, axis=1, keepdims=True)
        acc_ref[...] = alpha * acc_ref[...] + jnp.dot(p.astype(BF16), v_ref[0].astype(BF16),
                                                      preferred_element_type=F32)
        m_ref[...] = m_new

    @pl.when(j == nk - 1)
    def _():
        o_ref[0] = (acc_ref[...] / l_ref[...]).astype(o_ref.dtype)


def _fox_prompt(u3, qcol, cq, ck, hd):
    B, T, _ = u3.shape
    tq = tk = _pick(T, (512, 256))
    nq, nk = T // tq, T // tk
    kmap = lambda off: (lambda b, h, i, j: (b, jnp.minimum(j, (i * tq + tq - 1) // tk), off + h))
    return pl.pallas_call(
        functools.partial(_fox_flash_kernel, tq=tq, tk=tk, nk=nk, scale=hd ** -0.5),
        grid=(B, H_C, nq, nk),
        in_specs=[pl.BlockSpec((1, tq, hd), lambda b, h, i, j: (b, i, qcol + h)),
                  pl.BlockSpec((1, tk, hd), kmap(qcol + H_C)),
                  pl.BlockSpec((1, tk, hd), kmap(qcol + 2 * H_C)),
                  pl.BlockSpec((1, 1, tq, LANES), lambda b, h, i, j: (b, h, i, 0)),
                  pl.BlockSpec((1, 1, 1, tk),
                               lambda b, h, i, j: (b, h, 0, jnp.minimum(j, (i * tq + tq - 1) // tk)))],
        out_specs=pl.BlockSpec((1, tq, hd), lambda b, h, i, j: (b, i, h)),
        out_shape=jax.ShapeDtypeStruct((B, T, H_C * hd), BF16),
        scratch_shapes=[pltpu.VMEM((tq, LANES), F32), pltpu.VMEM((tq, LANES), F32),
                        pltpu.VMEM((tq, hd), F32)],
        compiler_params=_cp(("parallel", "parallel", "parallel", "arbitrary")),
        name="fox_prompt",
    )(u3, u3, u3, cq, ck)


def _fox_decay_kernel(pt_ref, *refs, nb):
    clf_refs, d_ref, tail_ref = refs[:nb], refs[nb], refs[nb + 1]
    j = pl.program_id(0)

    @pl.when(j == 0)
    def _():
        tail_ref[...] = jnp.zeros_like(tail_ref)

    lf = jnp.concatenate([r[0, 0] for r in clf_refs], axis=0)
    later = (_iota2((PAGE, PAGE), 0) > _iota2((PAGE, PAGE), 1)).astype(F32)
    d = jnp.dot(lf, later, precision=HI, preferred_element_type=F32) + tail_ref[...]
    tail_ref[...] = tail_ref[...] + jnp.dot(lf, jnp.ones((PAGE, PAGE), F32), precision=HI,
                                            preferred_element_type=F32)
    for b in range(nb):
        d_ref[b, 0] = d[H_C * b:H_C * (b + 1)]


def _fox_decay(clf_t, layer, page_table):
    B, npg = page_table.shape
    page = lambda b: (lambda j, pt: (layer, pt[b, npg - 1 - j], 0, 0))
    grid_spec = pltpu.PrefetchScalarGridSpec(
        num_scalar_prefetch=1,
        grid=(npg,),
        in_specs=[pl.BlockSpec((1, 1, H_C, PAGE), page(b)) for b in range(B)],
        out_specs=pl.BlockSpec((B, 1, H_C, PAGE), lambda j, pt: (0, npg - 1 - j, 0, 0)),
        scratch_shapes=[pltpu.VMEM((B * H_C, PAGE), F32)])
    return pl.pallas_call(
        functools.partial(_fox_decay_kernel, nb=B),
        grid_spec=grid_spec,
        out_shape=jax.ShapeDtypeStruct((B, npg, H_C, PAGE), F32),
        compiler_params=_cp(("arbitrary",)),
        name="fox_decay",
    )(page_table, *([clf_t] * B))


def _fox_sample_kernel(pt_ref, q_ref, kn_ref, vn_ref, f_ref, bf_ref, dp_ref, *refs, G, tn, nsteps, scale):
    k_refs, v_refs = refs[:G], refs[G:2 * G]
    o_ref, lfo_ref, m_ref, l_ref, acc_ref, cnq_ref = refs[2 * G:]
    j = pl.program_id(1)
    R = tn * H_C
    hd = q_ref.shape[-1]
    PW = PAGE * H_C
    rep = lambda x, n: jnp.concatenate([x] * n, axis=1)

    @pl.when(j == 0)
    def _():
        lane = _iota2((H_C, LANES), 1)
        lf = jnp.where(lane < R, _log_sigmoid(jnp.broadcast_to(f_ref[0], (H_C, LANES)) + bf_ref[...]), 0.0)
        lfo_ref[0] = lf[0:1]
        r_, c_ = _iota2((LANES, LANES), 0), _iota2((LANES, LANES), 1)
        mcum = ((r_ % H_C == c_ % H_C) & (r_ // H_C <= c_ // H_C)).astype(F32)
        c_row = jnp.dot(lf, mcum, precision=HI, preferred_element_type=F32)[0:1]
        diag = jnp.where(_iota2((R, LANES), 0) == _iota2((R, LANES), 1),
                         jnp.broadcast_to(c_row, (R, LANES)), 0.0)
        cnq = jnp.dot(diag, jnp.ones((LANES, LANES), F32), precision=HI, preferred_element_type=F32)
        cnq_ref[...] = cnq
        zpad = jnp.zeros((LANES - R, hd), F32)
        kn = jnp.concatenate([kn_ref[0], zpad], axis=0).astype(BF16)
        vn = jnp.concatenate([vn_ref[0], zpad], axis=0).astype(BF16)
        s = lax.dot_general(q_ref[0].astype(BF16), kn, (((1,), (1,)), ((), ())),
                            preferred_element_type=F32) * scale
        s = s + cnq - c_row
        row, col = _iota2((R, LANES), 0), _iota2((R, LANES), 1)
        ok = (row % H_C == col % H_C) & (col // H_C <= row // H_C)
        s = jnp.where(ok, s, -jnp.inf)
        m_new = jnp.max(s, axis=1, keepdims=True)
        p = jnp.exp(s - m_new)
        l_ref[...] = jnp.broadcast_to(jnp.sum(p, axis=1, keepdims=True), (R, LANES))
        acc_ref[...] = jnp.dot(p.astype(BF16), vn, preferred_element_type=F32)
        m_ref[...] = jnp.broadcast_to(m_new, (R, LANES))

    q = q_ref[0].astype(BF16)
    own = (_iota2((R, PW), 0) % H_C) == (_iota2((R, PW), 1) % H_C)
    cnq_w = rep(cnq_ref[...], PW // LANES)
    ss = []
    for g in range(G):
        s = lax.dot_general(q, k_refs[g][0, 0].astype(BF16), (((1,), (1,)), ((), ())),
                            preferred_element_type=F32) * scale
        ss.append(jnp.where(own, s + cnq_w + dp_ref[0, g], -jnp.inf))
    m_prev = m_ref[...]
    m_cur = jnp.max(ss[0], axis=1, keepdims=True)
    for g in range(1, G):
        m_cur = jnp.maximum(m_cur, jnp.max(ss[g], axis=1, keepdims=True))
    m_new = jnp.maximum(m_prev, m_cur)
    m_w = rep(m_new, PW // LANES)
    lsum, pv = None, None
    for g in range(G):
        p = jnp.exp(ss[g] - m_w)
        ls = jnp.sum(p, axis=1, keepdims=True)
        d = jnp.dot(p.astype(BF16), v_refs[g][0, 0].astype(BF16), preferred_element_type=F32)
        lsum = ls if lsum is None else lsum + ls
        pv = d if pv is None else pv + d
    alpha = jnp.exp(m_prev - m_new)
    l_ref[...] = alpha * l_ref[...] + lsum
    acc_ref[...] = alpha * acc_ref[...] + pv
    m_ref[...] = m_new

    @pl.when(j == nsteps - 1)
    def _():
        o_ref[0] = (acc_ref[...] / l_ref[...]).astype(o_ref.dtype)


def _fox_sample(qm, kn, vn, f_row, bf_row, dp, ck4, cv4, layer, page_table):
    B, R, hd = qm.shape
    tn = R // H_C
    npg = page_table.shape[1]
    G = _pick(npg, (8, 4, 2, 1))
    nsteps = npg // G
    PW = PAGE * H_C
    seq = lambda b, j, pt: (b, 0, 0)
    page = lambda g: (lambda b, j, pt: (layer, pt[b, G * j + g], 0, 0))
    grid_spec = pltpu.PrefetchScalarGridSpec(
        num_scalar_prefetch=1,
        grid=(B, nsteps),
        in_specs=[pl.BlockSpec((1, R, hd), seq), pl.BlockSpec((1, R, hd), seq), pl.BlockSpec((1, R, hd), seq),
                  pl.BlockSpec((1, 1, LANES), seq),
                  pl.BlockSpec((1, LANES), lambda b, j, pt: (0, 0)),
                  pl.BlockSpec((1, G, 1, PW), lambda b, j, pt: (b, j, 0, 0))]
                 + [pl.BlockSpec((1, 1, PW, hd), page(g)) for g in range(G)] * 2,
        out_specs=[pl.BlockSpec((1, R, hd), seq), pl.BlockSpec((1, 1, LANES), seq)],
        scratch_shapes=[pltpu.VMEM((R, LANES), F32), pltpu.VMEM((R, LANES), F32),
                        pltpu.VMEM((R, hd), F32), pltpu.VMEM((R, LANES), F32)])
    return pl.pallas_call(
        functools.partial(_fox_sample_kernel, G=G, tn=tn, nsteps=nsteps, scale=hd ** -0.5),
        grid_spec=grid_spec,
        out_shape=[jax.ShapeDtypeStruct((B, R, hd), BF16), jax.ShapeDtypeStruct((B, 1, LANES), F32)],
        compiler_params=_cp(("parallel", "arbitrary")),
        name="fox_sample",
    )(page_table, qm, kn, vn, f_row, bf_row, dp, *([ck4] * G), *([cv4] * G))


def _gla_kernel(q_ref, k_ref, v_ref, g_ref, ld_ref, aw_ref, ab_ref, ng_ref, s0_ref, y_ref, sout_ref,
                s_ref, o_ref, u_s, sb_s, *, tt, nt, t_valid):
    j = pl.program_id(2)
    dk = q_ref.shape[-1]
    C = GLA_CHUNK
    nch = tt // C

    @pl.when(j == 0)
    def _():
        s_ref[...] = s0_ref[0, 0]

    z = jnp.dot(ld_ref[0], aw_ref[...], precision=HI, preferred_element_type=F32) + ab_ref[...]
    lga = _log_sigmoid(z) * (1.0 / GLA_TAU)
    kx = k_ref[0]
    if t_valid is not None:
        valid = _iota2((tt, dk), 0) < t_valid
        lga = jnp.where(valid, lga, 0.0)
        kx = jnp.where(valid, kx, 0.0)
    same = (_iota2((tt, tt), 0) // C) == (_iota2((tt, tt), 1) // C)
    upto = _iota2((tt, tt), 0) >= _iota2((tt, tt), 1)
    b = jnp.dot((same & upto).astype(F32), lga, precision=HI, preferred_element_type=F32)
    bend = jnp.dot(same.astype(F32), lga, precision=HI, preferred_element_type=F32)
    qx = q_ref[0] * (dk ** -0.5)
    qe = qx * jnp.exp(b)
    kdt = (kx * jnp.exp(bend - b)).T
    kdt_h = kdt.astype(BF16)
    kdt_l = (kdt - kdt_h.astype(F32)).astype(BF16)
    dect = jnp.exp(bend).T
    v = v_ref[0]
    v_h = v.astype(BF16)
    v_l = (v - v_h.astype(F32)).astype(BF16)
    zero = jnp.zeros_like(v_h)
    rowc = _iota2((tt, v.shape[-1]), 0) // C
    tri_rows = _iota2((C, dk), 0)
    for c in range(nch):
        vm_h = jnp.where(rowc == c, v_h, zero)
        vm_l = jnp.where(rowc == c, v_l, zero)
        u_s[c] = (jnp.dot(kdt_h, vm_h, preferred_element_type=F32)
                  + jnp.dot(kdt_l, vm_h, preferred_element_type=F32)
                  + jnp.dot(kdt_h, vm_l, preferred_element_type=F32))
    S = s_ref[...]
    for c in range(nch):
        sb_s[c] = S.astype(BF16)
        S = S * dect[:, C * c:C * c + 1] + u_s[c]
    s_ref[...] = S
    for c in range(nch):
        r0 = C * c
        q_c, k_c, b_c, v_c = qx[r0:r0 + C], kx[r0:r0 + C], b[r0:r0 + C], v[r0:r0 + C]
        o_c = jnp.dot(qe[r0:r0 + C].astype(BF16), sb_s[c], preferred_element_type=F32)
        for s in range(C):
            rel = jnp.exp(jnp.where(tri_rows >= s, b_c - b_c[s:s + 1, :], -jnp.inf))
            att = jnp.sum(q_c * k_c[s:s + 1, :] * rel, axis=1, keepdims=True)
            o_c = o_c + att * v_c[s:s + 1, :]
        o_ref[r0:r0 + C, :] = o_c
    od = o_ref[...]
    od = od * lax.rsqrt(jnp.mean(od * od, axis=-1, keepdims=True) + LN_EPS)
    gate = g_ref[0]
    y_ref[0] = (od * ng_ref[...] * (gate * jax.nn.sigmoid(gate))).astype(y_ref.dtype)

    @pl.when(j == nt - 1)
    def _():
        sout_ref[0, 0] = s_ref[...]


def _gla_mixer(u3, qc, kc, vc, gc, ldc, aw_pad, ab, ng, s0, t_valid=None):
    B, Tp, _ = u3.shape
    _, H, dk, dv = s0.shape
    tt = _pick(Tp, (256, 128))
    nt = Tp // tt if t_valid is None else 1
    return pl.pallas_call(
        functools.partial(_gla_kernel, tt=tt, nt=nt, t_valid=t_valid),
        grid=(B, H, nt),
        in_specs=[pl.BlockSpec((1, tt, dk), lambda b, h, j: (b, j, qc + h)),
                  pl.BlockSpec((1, tt, dk), lambda b, h, j: (b, j, kc + h)),
                  pl.BlockSpec((1, tt, dv), lambda b, h, j: (b, j, vc + h)),
                  pl.BlockSpec((1, tt, dv), lambda b, h, j: (b, j, gc + h)),
                  pl.BlockSpec((1, tt, LANES), lambda b, h, j: (b, j, ldc)),
                  pl.BlockSpec((LANES, dk), lambda b, h, j: (0, h)),
                  pl.BlockSpec((1, dk), lambda b, h, j: (0, h)),
                  pl.BlockSpec((1, dv), lambda b, h, j: (0, h)),
                  pl.BlockSpec((1, 1, dk, dv), lambda b, h, j: (b, h, 0, 0))],
        out_specs=[pl.BlockSpec((1, tt, dv), lambda b, h, j: (b, j, h)),
                   pl.BlockSpec((1, 1, dk, dv), lambda b, h, j: (b, h, 0, 0))],
        out_shape=[jax.ShapeDtypeStruct((B, nt * tt, H * dv), BF16),
                   jax.ShapeDtypeStruct((B, H, dk, dv), F32)],
        scratch_shapes=[pltpu.VMEM((dk, dv), F32), pltpu.VMEM((tt, dv), F32),
                        pltpu.VMEM((tt // GLA_CHUNK, dk, dv), F32), pltpu.VMEM((tt // GLA_CHUNK, dk, dv), BF16)],
        compiler_params=_cp(("parallel", "parallel", "arbitrary")),
        name="gla",
    )(u3, u3, u3, u3, u3, aw_pad, ab.reshape(1, -1), ng.reshape(1, -1), s0)


def _pack_w_in_tail(w, fc, wd, dkq):
    D = w.shape[0]
    d0 = fc + H_C
    ld0 = d0 + 2 * dkq + 2 * wd
    n_ld = w.shape[1] - ld0
    return jnp.concatenate([w[:, d0:ld0], w[:, fc:d0], jnp.zeros((D, LANES - H_C), w.dtype),
                            w[:, ld0:], jnp.zeros((D, LANES - n_ld), w.dtype)], axis=1)


def _token_mixing(xb, B, T, P, layer, st, sample):
    M, D = xb.shape
    wa = wb = wc = D // 4
    wd = D - 3 * (D // 4)
    dkq = wd // 2
    hd = wc // H_C
    n_ld = P['gla_aw'].shape[0]
    cB = 2 * wa
    cC = cB + 3 * wb + 2 * LANES
    cD = cC + 3 * wc
    tm = _pick(M, (1024, 32))
    u = _matmul2(xb, P['w_in_all'], (layer,), cD, _pack_w_in_tail(P['w_in'], cD, wd, dkq), tm, 2 * LANES)
    N = u.shape[1]
    u3 = u.reshape(B, T, N)
    cF = cD + 2 * dkq + 2 * wd
    cL = cF + LANES
    bf_pad = jnp.concatenate([P['fox_bf'], jnp.zeros((LANES - H_C,), F32)])
    aw_pad = jnp.concatenate([P['gla_aw'], jnp.zeros((LANES - n_ld, dkq), F32)], axis=0)
    kc = u3[:, :, cC + wc:cC + 2 * wc].reshape(B, T, H_C, hd)
    vc = u3[:, :, cC + 2 * wc:cC + 3 * wc].reshape(B, T, H_C, hd)
    shift_new = u3[:, T - 1, cB:cB + 3 * wb + 2 * LANES]

    ya, conv_new = _conv_mixer(u3, st['conv'], P['conv_w'], P['conv_b'], P['conv_ln_g'], P['conv_ln_b'])

    if sample is None:
        up, t_valid = u3, None
    else:
        up, t_valid = jnp.pad(u3, ((0, 0), (0, RW_TILE - T), (0, 0))), T
    yb, wkv_new = _rwkv_mixer(up, cB // wb, st['shift'], st['wkv'], P, t_valid)
    yd, gla_new = _gla_mixer(up, cD // (dkq // H_D), (cD + dkq) // (dkq // H_D),
                             (cD + 2 * dkq) // (wd // H_D), (cD + 2 * dkq + wd) // (wd // H_D),
                             cL // LANES, aw_pad, P['gla_ab'], P['gla_ng'], st['gla'], t_valid)
    yb, yd = yb[:, :T], yd[:, :T]

    if sample is None:
        logf_pad, cq, ck = _fox_prep(u3, cF // LANES, bf_pad.reshape(1, LANES))
        logf = logf_pad[:, :, :H_C]
        yc = _fox_prompt(u3, cC // hd, cq, ck, hd)
    else:
        R = T * H_C
        rows = lambda c0: u3[:, :, c0:c0 + wc].reshape(B, R, hd)
        f_row = jnp.pad(u3[:, :, cF:cF + H_C].reshape(B, 1, R), ((0, 0), (0, 0), (0, LANES - R)))
        bf_row = jnp.pad(jnp.tile(P['fox_bf'], T), (0, LANES - R)).reshape(1, LANES)
        dpt = _fox_decay(sample['clf_t'], layer, sample['page_table'])
        dp = dpt.transpose(0, 1, 3, 2).reshape(B, dpt.shape[1], 1, PAGE * H_C)
        yc, lfo = _fox_sample(rows(cC), rows(cC + wc), rows(cC + 2 * wc), f_row, bf_row, dp,
                              sample['ck'], sample['cv'], layer, sample['page_table'])
        yc = yc.reshape(B, T, wc)
        logf = lfo[:, 0, :R].reshape(B, T, H_C)

    ymix = jnp.concatenate([ya, yb, yc, yd], axis=-1).reshape(M, D)
    y = _matmul(ymix, P['w_o_all'], (layer,), tm, _pick(D, (512,)), D)
    return y, (kc, vc, logf, conv_new, shift_new, wkv_new, gla_new)


def _swiglu(xb, P, layer, which):
    M, D = xb.shape
    F = P['ffn_wg_all'].shape[-1]
    widx = (layer, which)
    h = _ffn_up(xb, P['ffn_wg_all'], P['ffn_wu_all'], widx, _pick(M, (1024, 32)), _pick(F, (256,)))
    return _matmul(h, P['ffn_wd_all'], widx, _pick(M, (512, 32)), _pick(D, (256,)), F)


def _layer(x, xb, B, T, P, layer, st, sample, alpha):
    x, xb = _add_ln(x, _swiglu(xb, P, layer, 0), P['ln_g'][0], P['ln_b'][0], alpha, 0.5)
    y, new = _token_mixing(xb, B, T, P, layer, st, sample)
    x, xb = _add_ln(x, y, P['ln_g'][1], P['ln_b'][1], alpha, 1.0)
    x, xb = _add_ln(x, _swiglu(xb, P, layer, 1), P['ln_g'][2], P['ln_b'][2], alpha, 0.5)
    return x, xb, new


def kernel(x_prompt, x_sample, cache_k, cache_v, cache_logf, state_conv, state_shift, state_wkv, state_gla, page_table, ln_g, ln_b, ffn_wg, ffn_wu, ffn_wd, w_in, w_o, conv_w, conv_b, conv_ln_g, conv_ln_b, rwkv_mu, rwkv_w0, rwkv_w2, rwkv_a0, rwkv_a2, rwkv_g2, rwkv_kk, rwkv_ka, rwkv_rk, rwkv_gn_g, rwkv_gn_b, fox_bf, gla_aw, gla_ab, gla_ng):
    big = dict(ffn_wg_all=ffn_wg, ffn_wu_all=ffn_wu, ffn_wd_all=ffn_wd.astype(BF16), w_o_all=w_o,
               w_in_all=w_in)
    stacked = dict(ln_g=ln_g, ln_b=ln_b, w_in=w_in, conv_w=conv_w, conv_b=conv_b, conv_ln_g=conv_ln_g,
                   conv_ln_b=conv_ln_b, rwkv_mu=rwkv_mu, rwkv_w0=rwkv_w0, rwkv_w2=rwkv_w2, rwkv_a0=rwkv_a0,
                   rwkv_a2=rwkv_a2, rwkv_g2=rwkv_g2, rwkv_kk=rwkv_kk, rwkv_ka=rwkv_ka, rwkv_rk=rwkv_rk,
                   rwkv_gn_g=rwkv_gn_g, rwkv_gn_b=rwkv_gn_b, fox_bf=fox_bf, gla_aw=gla_aw, gla_ab=gla_ab,
                   gla_ng=gla_ng)
    depth = w_in.shape[0]
    Bp, Tp, D = x_prompt.shape
    Bs, Ts, _ = x_sample.shape
    alpha = (2.0 * depth) ** 0.25
    n_pool, page, hc, hdc = cache_k.shape[1:]
    ck = cache_k.reshape(depth, n_pool, page * hc, hdc)
    cv = cache_v.reshape(depth, n_pool, page * hc, hdc)
    clf_t = cache_logf.transpose(0, 1, 3, 2)
    xp, xs = x_prompt.reshape(Bp * Tp, D), x_sample.reshape(Bs * Ts, D)
    xpb, xsb = xp.astype(BF16), xs.astype(BF16)
    zero_state = dict(conv=jnp.zeros((Bp,) + state_conv.shape[2:], F32),
                      shift=jnp.zeros((Bp,) + state_shift.shape[2:], F32),
                      wkv=jnp.zeros((Bp,) + state_wkv.shape[2:], F32),
                      gla=jnp.zeros((Bp,) + state_gla.shape[2:], F32))
    st_p, st_s = [], []
    for l in range(depth):
        P = {n: a[l] for n, a in stacked.items()}
        P.update(big)
        xp, xpb, s = _layer(xp, xpb, Bp, Tp, P, l, zero_state, None, alpha)
        st_p.append(s)
        st = dict(conv=state_conv[l], shift=state_shift[l], wkv=state_wkv[l], gla=state_gla[l])
        xs, xsb, s = _layer(xs, xsb, Bs, Ts, P, l, st,
                            dict(ck=ck, cv=cv, clf_t=clf_t, page_table=page_table), alpha)
        st_s.append(s)
    stk = lambda sts, i: jnp.stack([s[i] for s in sts])
    return ((xp.reshape(Bp, Tp, D), xs.reshape(Bs, Ts, D))
            + tuple(stk(st_p, i) for i in range(7)) + tuple(stk(st_s, i) for i in range(7)))
```

```python
import functools

import numpy as np
import jax
import jax.numpy as jnp
from jax import lax
from jax.experimental import pallas as pl
from jax.experimental.pallas import tpu as pltpu

F32 = jnp.float32
BF16 = jnp.bfloat16
HI = lax.Precision.HIGHEST

LANES = 128
VMEM_LIMIT = 56 * 1024 * 1024

LN_EPS = 1e-5
GN_EPS = 64e-5
CONV_K = 31
HD_B = 64
H_C = 8
H_D = 4
GLA_TAU = 16.0
GLA_CHUNK = 16
PAGE = 128
RW_TILE = 128
RW_SUB = 16


def _cp(sem):
    return pltpu.CompilerParams(dimension_semantics=sem, vmem_limit_bytes=VMEM_LIMIT)


def _pick(n, prefs):
    for p in prefs:
        if n % p == 0:
            return p
    return n


def _log_sigmoid(x):
    return jnp.minimum(x, 0.0) - jnp.log1p(jnp.exp(-jnp.abs(x)))


def _softplus(x):
    return jnp.maximum(x, 0.0) + jnp.log1p(jnp.exp(-jnp.abs(x)))


def _split_dot(x, w_bf16, n):
    acc = None
    rem = x
    for i in range(n):
        h = rem.astype(BF16)
        d = jnp.dot(h, w_bf16, preferred_element_type=F32)
        acc = d if acc is None else acc + d
        if i < n - 1:
            rem = rem - h.astype(F32)
    return acc


def _iota2(shape, dim):
    return lax.broadcasted_iota(jnp.int32, shape, dim)


def _mm_kernel(x_ref, w_ref, o_ref, acc_ref, *, nk):
    part = jnp.dot(x_ref[...].astype(BF16), w_ref[...].astype(BF16), preferred_element_type=F32)
    if nk == 1:
        o_ref[...] = part.astype(o_ref.dtype)
        return
    k = pl.program_id(2)

    @pl.when(k == 0)
    def _():
        acc_ref[...] = part

    @pl.when(k > 0)
    def _():
        acc_ref[...] += part

    @pl.when(k == nk - 1)
    def _():
        o_ref[...] = acc_ref[...].astype(o_ref.dtype)


def _matmul(x, w, widx, tm, tn, tk, out_dtype=F32):
    M, K = x.shape
    N = w.shape[-1]
    nk = K // tk
    return pl.pallas_call(
        functools.partial(_mm_kernel, nk=nk),
        grid=(M // tm, N // tn, nk),
        in_specs=[pl.BlockSpec((tm, tk), lambda i, j, k: (i, k)),
                  pl.BlockSpec((None,) * len(widx) + (tk, tn), lambda i, j, k: widx + (k, j))],
        out_specs=pl.BlockSpec((tm, tn), lambda i, j, k: (i, j)),
        out_shape=jax.ShapeDtypeStruct((M, N), out_dtype),
        scratch_shapes=[pltpu.VMEM((tm, tn) if nk > 1 else (8, LANES), F32)],
        compiler_params=_cp(("parallel", "parallel", "arbitrary")),
        name="matmul",
    )(x, w)


def _ffn_up_kernel(x_ref, wg_ref, wu_ref, o_ref):
    x = x_ref[...]
    g = jnp.dot(x, wg_ref[...].astype(BF16), preferred_element_type=F32)
    u = jnp.dot(x, wu_ref[...].astype(BF16), preferred_element_type=F32)
    o_ref[...] = (g * jax.nn.sigmoid(g) * u).astype(o_ref.dtype)


def _ffn_up(xb, wg, wu, widx, tm, tn):
    M, K = xb.shape
    N = wg.shape[-1]
    wspec = pl.BlockSpec((None,) * len(widx) + (K, tn), lambda i, j: widx + (0, j))
    return pl.pallas_call(
        _ffn_up_kernel,
        grid=(M // tm, N // tn),
        in_specs=[pl.BlockSpec((tm, K), lambda i, j: (i, 0)), wspec, wspec],
        out_specs=pl.BlockSpec((tm, tn), lambda i, j: (i, j)),
        out_shape=jax.ShapeDtypeStruct((M, N), BF16),
        compiler_params=_cp(("parallel", "arbitrary")),
        name="ffn_up",
    )(xb, wg, wu)


def _add_ln_kernel(x_ref, y_ref, g_ref, b_ref, of_ref, ob_ref, *, alpha, c):
    z = alpha * x_ref[...] + c * y_ref[...]
    mu = jnp.mean(z, axis=-1, keepdims=True)
    zc = z - mu
    var = jnp.mean(zc * zc, axis=-1, keepdims=True)
    o = zc * lax.rsqrt(var + LN_EPS) * g_ref[...] + b_ref[...]
    of_ref[...] = o
    ob_ref[...] = o.astype(BF16)


def _add_ln(x, y, g, b, alpha, c):
    M, D = x.shape
    tr = _pick(M, (256, 32))
    return pl.pallas_call(
        functools.partial(_add_ln_kernel, alpha=alpha, c=c),
        grid=(M // tr,),
        in_specs=[pl.BlockSpec((tr, D), lambda i: (i, 0)),
                  pl.BlockSpec((tr, D), lambda i: (i, 0)),
                  pl.BlockSpec((1, D), lambda i: (0, 0)),
                  pl.BlockSpec((1, D), lambda i: (0, 0))],
        out_specs=[pl.BlockSpec((tr, D), lambda i: (i, 0)),
                   pl.BlockSpec((tr, D), lambda i: (i, 0))],
        out_shape=[jax.ShapeDtypeStruct((M, D), F32), jax.ShapeDtypeStruct((M, D), BF16)],
        compiler_params=_cp(("parallel",)),
        name="add_ln",
    )(x, y, g.reshape(1, D), b.reshape(1, D))


def _conv_kernel(val_ref, gate_ref, buf_ref, w_ref, cb_ref, g_ref, b_ref, y_ref, new_ref, zz_ref,
                 *, tt, nt):
    j = pl.program_id(1)
    hist = CONV_K - 1

    @pl.when(j == 0)
    def _():
        zz_ref[2:2 + hist, :] = buf_ref[0]

    zz_ref[32:32 + tt, :] = val_ref[0] * jax.nn.sigmoid(gate_ref[0])
    rc = min(tt, 32)
    for r0 in range(0, tt, rc):
        acc = None
        for k in range(CONV_K):
            term = w_ref[k:k + 1, :] * zz_ref[2 + k + r0:2 + k + r0 + rc, :]
            acc = term if acc is None else acc + term
        y = acc + cb_ref[...]
        mu = jnp.mean(y, axis=-1, keepdims=True)
        yc = y - mu
        var = jnp.mean(yc * yc, axis=-1, keepdims=True)
        yn = yc * lax.rsqrt(var + LN_EPS) * g_ref[...] + b_ref[...]
        y_ref[0, r0:r0 + rc, :] = (yn * jax.nn.sigmoid(yn)).astype(y_ref.dtype)
    tail = zz_ref[2 + tt:2 + tt + hist, :]
    zz_ref[2:2 + hist, :] = tail

    @pl.when(j == nt - 1)
    def _():
        new_ref[0] = tail


def _conv_mixer(u3, buf, cw, cb, lg, lb):
    B, T, _ = u3.shape
    W = buf.shape[-1]
    tt = _pick(T, (256,))
    nt = T // tt
    return pl.pallas_call(
        functools.partial(_conv_kernel, tt=tt, nt=nt),
        grid=(B, nt),
        in_specs=[pl.BlockSpec((1, tt, W), lambda b, j: (b, j, 0)),
                  pl.BlockSpec((1, tt, W), lambda b, j: (b, j, 1)),
                  pl.BlockSpec((1, CONV_K - 1, W), lambda b, j: (b, 0, 0)),
                  pl.BlockSpec((CONV_K, W), lambda b, j: (0, 0)),
                  pl.BlockSpec((1, W), lambda b, j: (0, 0)),
                  pl.BlockSpec((1, W), lambda b, j: (0, 0)),
                  pl.BlockSpec((1, W), lambda b, j: (0, 0))],
        out_specs=[pl.BlockSpec((1, tt, W), lambda b, j: (b, j, 0)),
                   pl.BlockSpec((1, CONV_K - 1, W), lambda b, j: (b, 0, 0))],
        out_shape=[jax.ShapeDtypeStruct((B, T, W), BF16),
                   jax.ShapeDtypeStruct((B, CONV_K - 1, W), F32)],
        scratch_shapes=[pltpu.VMEM((32 + tt, W), F32)],
        compiler_params=_cp(("parallel", "arbitrary")),
        name="conv_mixer",
    )(u3, u3, buf, cw, cb.reshape(1, W), lg.reshape(1, W), lb.reshape(1, W))


def _rwkv_kernel(r_ref, k_ref, v_ref, lwa_ref, lg_ref, shift_ref, s0_ref, mu_ref, w0_ref, w2h_ref, w2l_ref,
                 a0_ref, a2h_ref, a2l_ref, g2_ref, kkp_ref, ka_ref, rk_ref, gng_ref, gnb_ref, sel_ref,
                 y_ref, sout_ref,
                 s_ref, prev_ref, kk_s, d_s, bv_s, k2_s, rp_s, v_s, rho_s, rks_s, g_s, lhh_s, lhl_s,
                 vbuf_s, ybuf_s, *, G, nt, t_valid):
    j = pl.program_id(1)
    TT = RW_TILE
    WB = r_ref.shape[-1]
    npair = WB // LANES

    @pl.when(j == 0)
    def _():
        s_ref[...] = s0_ref[...]
        prev_ref[...] = shift_ref[...]

    ob = (_iota2((LANES, LANES), 0) // HD_B == _iota2((LANES, LANES), 1) // HD_B).astype(BF16)
    obd = (_iota2((2 * LANES, 2 * LANES), 0) // HD_B == _iota2((2 * LANES, 2 * LANES), 1) // HD_B).astype(BF16)

    def segsum(x):
        rows = x.shape[0]
        tall = jnp.concatenate([x[:, LANES * p:LANES * (p + 1)] for p in range(npair)], axis=0)
        s = _split_dot(tall, ob, 2)
        return jnp.concatenate([s[rows * p:rows * (p + 1)] for p in range(npair)], axis=1)

    def dot3(x, wh_ref, wl_ref):
        xh = x.astype(BF16)
        xl = (x - xh.astype(F32)).astype(BF16)
        wh = wh_ref[...]
        return (jnp.dot(xh, wh, preferred_element_type=F32) + jnp.dot(xl, wh, preferred_element_type=F32)
                + jnp.dot(xh, wl_ref[...], preferred_element_type=F32))

    row0 = _iota2((TT, 1), 0) == 0
    last = (t_valid if t_valid is not None else TT) - 1
    for gi in range(G):
        def shifted(x, lo, hi, gi=gi):
            prev = jnp.where(row0, prev_ref[gi, :, lo:hi], pltpu.roll(x, 1, 0))
            return x + (prev - x) * mu_ref[:, lo:hi]

        raw = (r_ref[gi], k_ref[gi], v_ref[gi], lwa_ref[gi], lg_ref[gi])
        offs = (0, WB, 2 * WB, 3 * WB, 3 * WB + LANES, 3 * WB + 2 * LANES)
        r, k, v, lwa, lg = [shifted(x, offs[i], offs[i + 1]) for i, x in enumerate(raw)]
        for i, x in enumerate(raw):
            prev_ref[gi, :, offs[i]:offs[i + 1]] = x[last:last + 1]

        w = -_softplus(-(w0_ref[...] + dot3(jnp.tanh(lwa), w2h_ref, w2l_ref))) - 0.5
        d = jnp.exp(-jnp.exp(w))
        a = jax.nn.sigmoid(a0_ref[...] + dot3(lwa, a2h_ref, a2l_ref))
        g = jnp.dot(jax.nn.sigmoid(lg).astype(BF16), g2_ref[...], preferred_element_type=F32)
        kk = k * kkp_ref[...]
        kk = kk * lax.rsqrt(jnp.maximum(segsum(kk * kk), 1e-24))
        k2 = k * (1.0 + (a - 1.0) * ka_ref[...])
        bv = kk * a
        kk_s[gi] = kk
        d_s[gi] = d
        bv_s[gi] = bv
        k2_s[gi] = k2
        rp_s[gi] = d * r - kk * segsum(bv * r)
        v_s[gi] = v
        rho_s[gi] = segsum(k2 * r)
        rks_s[gi] = segsum(r * k2 * rk_ref[...])
        g_s[gi] = g

        vt = v.T
        vh = vt.astype(BF16).astype(F32)
        vl = vt - vh
        for p in range(npair):
            for h in range(2):
                src = slice(LANES * p + HD_B * h, LANES * p + HD_B * (h + 1))
                lhh_s[gi, HD_B * p:HD_B * (p + 1), TT * h:TT * (h + 1)] = vh[src, :]
                lhl_s[gi, HD_B * p:HD_B * (p + 1), TT * h:TT * (h + 1)] = vl[src, :]

    ybuf_s[...] = jnp.zeros_like(ybuf_s)
    lane_x = _iota2((npair * HD_B, LANES), 1)
    lane_c = lane_x % HD_B
    n_steps = TT if t_valid is None else t_valid
    n_sub = -(-n_steps // RW_SUB)
    for sub in range(n_sub):
        base = (TT - RW_SUB * sub) % TT
        for gi in range(G):
            parts = [lhh_s[gi, :, 0:TT], lhh_s[gi, :, TT:2 * TT], lhl_s[gi, :, 0:TT], lhl_s[gi, :, TT:2 * TT]]
            x = 0.0
            for q in range(3, -1, -1):
                sh = (base + RW_SUB * q) % TT
                part = pltpu.roll(parts[q], sh, 1) if sh else parts[q]
                x = jnp.where(lane_x < RW_SUB * (q + 1), part, x)
            xb = x.astype(BF16)
            for t2 in range(RW_SUB // 2):
                res = jnp.dot(xb, sel_ref[:, 2 * LANES * t2:2 * LANES * (t2 + 1)], preferred_element_type=F32)
                vbuf_s[gi, 2 * t2] = res[:, 0:LANES]
                vbuf_s[gi, 2 * t2 + 1] = res[:, LANES:2 * LANES]

        def step(tl, carry, sub=sub):
            t = RW_SUB * sub + tl
            half = t // HD_B
            mask = lane_c == (t % HD_B)

            def tile(ref, gi):
                row = ref[gi, pl.ds(t, 1), :]
                return jnp.concatenate(
                    [jnp.broadcast_to(row[:, LANES * p:LANES * (p + 1)], (HD_B, LANES)) for p in range(npair)],
                    axis=0)

            for gi in range(G):
                S = s_ref[gi]
                lhs = jnp.concatenate([(S * tile(kk_s, gi)).astype(BF16), (S * tile(rp_s, gi)).astype(BF16)],
                                      axis=1)
                res = jnp.dot(lhs, obd, preferred_element_type=F32)
                s_ref[gi] = (S * tile(d_s, gi) - res[:, 0:LANES] * tile(bv_s, gi)
                             + vbuf_s[gi, tl] * tile(k2_s, gi))
                ybuf_s[gi, half] = jnp.where(mask, res[:, LANES:2 * LANES], ybuf_s[gi, half])
            return carry

        lax.fori_loop(0, min(RW_SUB, n_steps - RW_SUB * sub), step, 0, unroll=4)

    for gi in range(G):
        halves = []
        for hf in range(TT // HD_B):
            yt = ybuf_s[gi, hf].T
            cols = []
            for p in range(npair):
                cols.append(yt[0:HD_B, HD_B * p:HD_B * (p + 1)])
                cols.append(yt[HD_B:2 * HD_B, HD_B * p:HD_B * (p + 1)])
            halves.append(jnp.concatenate(cols, axis=1))
        v = v_s[gi]
        y = jnp.concatenate(halves, axis=0) + v * rho_s[gi]
        m = segsum(y) * (1.0 / HD_B)
        yc = y - m
        var = segsum(yc * yc) * (1.0 / HD_B)
        yn = yc * lax.rsqrt(var + GN_EPS) * gng_ref[...] + gnb_ref[...]
        y_ref[gi] = ((yn + rks_s[gi] * v) * g_s[gi]).astype(y_ref.dtype)

    @pl.when(j == nt - 1)
    def _():
        sout_ref[...] = s_ref[...]


def _rwkv_sel():
    rows = np.arange(LANES)
    cols = np.arange(RW_SUB * LANES)
    hit = (rows[:, None] < 4 * RW_SUB) & \
          (((rows[:, None] % (2 * RW_SUB)) // RW_SUB) == ((cols[None, :] % LANES) // HD_B)) & \
          ((rows[:, None] % RW_SUB) == (cols[None, :] // LANES))
    return jnp.asarray(hit, dtype=BF16)


def _rwkv_mixer(u3, cb, shift, wkv, P, t_valid=None):
    B, Tp, _ = u3.shape
    WB = P['rwkv_w0'].shape[-1]
    H = WB // HD_B
    npair = H // 2
    G = 2 if B % 2 == 0 else 1
    nt = Tp // RW_TILE if t_valid is None else 1
    NB = 3 * WB + 2 * LANES
    lb = (cb + 3) * (WB // LANES)
    s0 = wkv.reshape(B, npair, 2, HD_B, HD_B).transpose(0, 1, 3, 2, 4).reshape(B, npair * HD_B, LANES)
    zpad = jnp.zeros((HD_B, WB), F32)
    w2 = jnp.concatenate([P['rwkv_w2'], zpad], axis=0)
    a2 = jnp.concatenate([zpad, P['rwkv_a2']], axis=0)
    hi = lambda w: w.astype(BF16)
    lo = lambda w: (w - w.astype(BF16).astype(F32)).astype(BF16)
    row = lambda x: x.reshape(1, -1)
    full = lambda shape: pl.BlockSpec(shape, lambda b, j: (0,) * len(shape))
    tok = lambda w, c: pl.BlockSpec((G, RW_TILE, w), lambda b, j, c=c: (b, j, c))
    SR = npair * HD_B
    y, snew = pl.pallas_call(
        functools.partial(_rwkv_kernel, G=G, nt=nt, t_valid=t_valid),
        grid=(B // G, nt),
        in_specs=[tok(WB, cb), tok(WB, cb + 1), tok(WB, cb + 2), tok(LANES, lb), tok(LANES, lb + 1),
                  pl.BlockSpec((G, 1, NB), lambda b, j: (b, 0, 0)),
                  pl.BlockSpec((G, SR, LANES), lambda b, j: (b, 0, 0)),
                  full((1, NB)), full((1, WB)), full((LANES, WB)), full((LANES, WB)), full((1, WB)),
                  full((LANES, WB)), full((LANES, WB)), full((LANES, WB)), full((1, WB)), full((1, WB)),
                  full((1, WB)), full((1, WB)), full((1, WB)), full((LANES, RW_SUB * LANES))],
        out_specs=[pl.BlockSpec((G, RW_TILE, WB), lambda b, j: (b, j, 0)),
                   pl.BlockSpec((G, SR, LANES), lambda b, j: (b, 0, 0))],
        out_shape=[jax.ShapeDtypeStruct((B, nt * RW_TILE, WB), BF16),
                   jax.ShapeDtypeStruct((B, SR, LANES), F32)],
        scratch_shapes=[pltpu.VMEM((G, SR, LANES), F32),
                        pltpu.VMEM((G, 1, NB), F32)]
                       + [pltpu.VMEM((G, RW_TILE, WB), F32)] * 9
                       + [pltpu.VMEM((G, SR, 2 * RW_TILE), F32)] * 2
                       + [pltpu.VMEM((G, RW_SUB, SR, LANES), F32),
                          pltpu.VMEM((G, RW_TILE // HD_B, SR, LANES), F32)],
        compiler_params=_cp(("parallel", "arbitrary")),
        name="rwkv7",
    )(u3, u3, u3, u3, u3, shift.reshape(B, 1, NB), s0, row(P['rwkv_mu']), row(P['rwkv_w0']), hi(w2), lo(w2),
      row(P['rwkv_a0']), hi(a2), lo(a2), hi(P['rwkv_g2']), row(P['rwkv_kk']), row(P['rwkv_ka']),
      row(P['rwkv_rk']), row(P['rwkv_gn_g']), row(P['rwkv_gn_b']), _rwkv_sel())
    snew = snew.reshape(B, npair, HD_B, 2, HD_B).transpose(0, 1, 3, 2, 4).reshape(B, H, HD_B, HD_B)
    return y, snew


def _fox_prep_kernel(f_ref, bf_ref, logf_ref, cq_ref, ck_ref, carry_ref, *, tc):
    j = pl.program_id(1)

    @pl.when(j == 0)
    def _():
        carry_ref[...] = jnp.zeros_like(carry_ref)

    lf = _log_sigmoid(f_ref[0] + bf_ref[...])
    logf_ref[0] = lf
    tri = (_iota2((tc, tc), 0) >= _iota2((tc, tc), 1)).astype(F32)
    c = jnp.dot(tri, lf, precision=HI, preferred_element_type=F32) + carry_ref[...]
    carry_ref[...] = c[tc - 1:tc, :]
    ct = c.T
    for h in range(H_C):
        cq_ref[0, h] = jnp.broadcast_to(c[:, h:h + 1], (tc, LANES))
        ck_ref[0, h] = ct[h:h + 1, :]


def _fox_prep(u3, fcol, bf_pad):
    B, T, _ = u3.shape
    tc = _pick(T, (256,))
    return pl.pallas_call(
        functools.partial(_fox_prep_kernel, tc=tc),
        grid=(B, T // tc),
        in_specs=[pl.BlockSpec((1, tc, LANES), lambda b, j: (b, j, fcol)),
                  pl.BlockSpec((1, LANES), lambda b, j: (0, 0))],
        out_specs=[pl.BlockSpec((1, tc, LANES), lambda b, j: (b, j, 0)),
                   pl.BlockSpec((1, H_C, tc, LANES), lambda b, j: (b, 0, j, 0)),
                   pl.BlockSpec((1, H_C, 1, tc), lambda b, j: (b, 0, 0, j))],
        out_shape=[jax.ShapeDtypeStruct((B, T, LANES), F32),
                   jax.ShapeDtypeStruct((B, H_C, T, LANES), F32),
                   jax.ShapeDtypeStruct((B, H_C, 1, T), F32)],
        scratch_shapes=[pltpu.VMEM((1, LANES), F32)],
        compiler_params=_cp(("parallel", "arbitrary")),
        name="fox_prep",
    )(u3, bf_pad)


def _fox_flash_kernel(q_ref, k_ref, v_ref, cq_ref, ck_ref, o_ref, m_ref, l_ref, acc_ref,
                      *, tq, tk, nk, scale):
    i = pl.program_id(2)
    j = pl.program_id(3)

    @pl.when(j == 0)
    def _():
        m_ref[...] = jnp.full_like(m_ref, -jnp.inf)
        l_ref[...] = jnp.zeros_like(l_ref)
        acc_ref[...] = jnp.zeros_like(acc_ref)

    @pl.when(j * tk <= i * tq + tq - 1)
    def _():
        s = lax.dot_general(q_ref[0].astype(BF16), k_ref[0].astype(BF16), (((1,), (1,)), ((), ())),
                            preferred_element_type=F32) * scale
        rep = tk // LANES
        s = s + jnp.concatenate([cq_ref[0, 0]] * rep, axis=1) - ck_ref[0, 0]
        qpos = i * tq + _iota2((tq, tk), 0)
        kpos = j * tk + _iota2((tq, tk), 1)
        s = jnp.where(kpos <= qpos, s, -jnp.inf)
        m_prev = m_ref[...]
        m_new = jnp.maximum(m_prev, jnp.max(s, axis=1, keepdims=True))
        p = jnp.exp(s - jnp.concatenate([m_new] * rep, axis=1))
        alpha = jnp.exp(m_prev - m_new)
        l_ref[...] = alpha * l_ref[...] + jnp.sum(p, axis=1, keepdims=True)
        acc_ref[...] = alpha * acc_ref[...] + jnp.dot(p.astype(BF16), v_ref[0].astype(BF16),
                                                      preferred_element_type=F32)
        m_ref[...] = m_new

    @pl.when(j == nk - 1)
    def _():
        o_ref[0] = (acc_ref[...] / l_ref[...]).astype(o_ref.dtype)


def _fox_prompt(u3, qcol, cq, ck, hd):
    B, T, _ = u3.shape
    tq = tk = _pick(T, (512, 256))
    nq, nk = T // tq, T // tk
    kmap = lambda off: (lambda b, h, i, j: (b, jnp.minimum(j, (i * tq + tq - 1) // tk), off + h))
    return pl.pallas_call(
        functools.partial(_fox_flash_kernel, tq=tq, tk=tk, nk=nk, scale=hd ** -0.5),
        grid=(B, H_C, nq, nk),
        in_specs=[pl.BlockSpec((1, tq, hd), lambda b, h, i, j: (b, i, qcol + h)),
                  pl.BlockSpec((1, tk, hd), kmap(qcol + H_C)),
                  pl.BlockSpec((1, tk, hd), kmap(qcol + 2 * H_C)),
                  pl.BlockSpec((1, 1, tq, LANES), lambda b, h, i, j: (b, h, i, 0)),
                  pl.BlockSpec((1, 1, 1, tk),
                               lambda b, h, i, j: (b, h, 0, jnp.minimum(j, (i * tq + tq - 1) // tk)))],
        out_specs=pl.BlockSpec((1, tq, hd), lambda b, h, i, j: (b, i, h)),
        out_shape=jax.ShapeDtypeStruct((B, T, H_C * hd), BF16),
        scratch_shapes=[pltpu.VMEM((tq, LANES), F32), pltpu.VMEM((tq, LANES), F32),
                        pltpu.VMEM((tq, hd), F32)],
        compiler_params=_cp(("parallel", "parallel", "parallel", "arbitrary")),
        name="fox_prompt",
    )(u3, u3, u3, cq, ck)


def _fox_decay_kernel(pt_ref, *refs, nb):
    clf_refs, d_ref, tail_ref = refs[:nb], refs[nb], refs[nb + 1]
    j = pl.program_id(0)

    @pl.when(j == 0)
    def _():
        tail_ref[...] = jnp.zeros_like(tail_ref)

    lf = jnp.concatenate([r[0, 0] for r in clf_refs], axis=0)
    later = (_iota2((PAGE, PAGE), 0) > _iota2((PAGE, PAGE), 1)).astype(F32)
    d = jnp.dot(lf, later, precision=HI, preferred_element_type=F32) + tail_ref[...]
    tail_ref[...] = tail_ref[...] + jnp.dot(lf, jnp.ones((PAGE, PAGE), F32), precision=HI,
                                            preferred_element_type=F32)
    for b in range(nb):
        d_ref[b, 0] = d[H_C * b:H_C * (b + 1)]


def _fox_decay(clf_t, layer, page_table):
    B, npg = page_table.shape
    page = lambda b: (lambda j, pt: (layer, pt[b, npg - 1 - j], 0, 0))
    grid_spec = pltpu.PrefetchScalarGridSpec(
        num_scalar_prefetch=1,
        grid=(npg,),
        in_specs=[pl.BlockSpec((1, 1, H_C, PAGE), page(b)) for b in range(B)],
        out_specs=pl.BlockSpec((B, 1, H_C, PAGE), lambda j, pt: (0, npg - 1 - j, 0, 0)),
        scratch_shapes=[pltpu.VMEM((B * H_C, PAGE), F32)])
    return pl.pallas_call(
        functools.partial(_fox_decay_kernel, nb=B),
        grid_spec=grid_spec,
        out_shape=jax.ShapeDtypeStruct((B, npg, H_C, PAGE), F32),
        compiler_params=_cp(("arbitrary",)),
        name="fox_decay",
    )(page_table, *([clf_t] * B))


def _fox_sample_kernel(pt_ref, q_ref, kn_ref, vn_ref, f_ref, bf_ref, dp_ref, *refs, G, tn, nsteps, scale):
    k_refs, v_refs = refs[:G], refs[G:2 * G]
    o_ref, lfo_ref, m_ref, l_ref, acc_ref, cnq_ref = refs[2 * G:]
    j = pl.program_id(1)
    R = tn * H_C
    hd = q_ref.shape[-1]
    PW = PAGE * H_C
    rep = lambda x, n: jnp.concatenate([x] * n, axis=1)

    @pl.when(j == 0)
    def _():
        lane = _iota2((H_C, LANES), 1)
        lf = jnp.where(lane < R, _log_sigmoid(jnp.broadcast_to(f_ref[0], (H_C, LANES)) + bf_ref[...]), 0.0)
        lfo_ref[0] = lf[0:1]
        r_, c_ = _iota2((LANES, LANES), 0), _iota2((LANES, LANES), 1)
        mcum = ((r_ % H_C == c_ % H_C) & (r_ // H_C <= c_ // H_C)).astype(F32)
        c_row = jnp.dot(lf, mcum, precision=HI, preferred_element_type=F32)[0:1]
        diag = jnp.where(_iota2((R, LANES), 0) == _iota2((R, LANES), 1),
                         jnp.broadcast_to(c_row, (R, LANES)), 0.0)
        cnq = jnp.dot(diag, jnp.ones((LANES, LANES), F32), precision=HI, preferred_element_type=F32)
        cnq_ref[...] = cnq
        zpad = jnp.zeros((LANES - R, hd), F32)
        kn = jnp.concatenate([kn_ref[0], zpad], axis=0).astype(BF16)
        vn = jnp.concatenate([vn_ref[0], zpad], axis=0).astype(BF16)
        s = lax.dot_general(q_ref[0].astype(BF16), kn, (((1,), (1,)), ((), ())),
                            preferred_element_type=F32) * scale
        s = s + cnq - c_row
        row, col = _iota2((R, LANES), 0), _iota2((R, LANES), 1)
        ok = (row % H_C == col % H_C) & (col // H_C <= row // H_C)
        s = jnp.where(ok, s, -jnp.inf)
        m_new = jnp.max(s, axis=1, keepdims=True)
        p = jnp.exp(s - m_new)
        l_ref[...] = jnp.broadcast_to(jnp.sum(p, axis=1, keepdims=True), (R, LANES))
        acc_ref[...] = jnp.dot(p.astype(BF16), vn, preferred_element_type=F32)
        m_ref[...] = jnp.broadcast_to(m_new, (R, LANES))

    q = q_ref[0].astype(BF16)
    own = (_iota2((R, PW), 0) % H_C) == (_iota2((R, PW), 1) % H_C)
    cnq_w = rep(cnq_ref[...], PW // LANES)
    ss = []
    for g in range(G):
        s = lax.dot_general(q, k_refs[g][0, 0].astype(BF16), (((1,), (1,)), ((), ())),
                            preferred_element_type=F32) * scale
        ss.append(jnp.where(own, s + cnq_w + dp_ref[0, g], -jnp.inf))
    m_prev = m_ref[...]
    m_cur = jnp.max(ss[0], axis=1, keepdims=True)
    for g in range(1, G):
        m_cur = jnp.maximum(m_cur, jnp.max(ss[g], axis=1, keepdims=True))
    m_new = jnp.maximum(m_prev, m_cur)
    m_w = rep(m_new, PW // LANES)
    lsum, pv = None, None
    for g in range(G):
        p = jnp.exp(ss[g] - m_w)
        ls = jnp.sum(p, axis=1, keepdims=True)
        d = jnp.dot(p.astype(BF16), v_refs[g][0, 0].astype(BF16), preferred_element_type=F32)
        lsum = ls if lsum is None else lsum + ls
        pv = d if pv is None else pv + d
    alpha = jnp.exp(m_prev - m_new)
    l_ref[...] = alpha * l_ref[...] + lsum
    acc_ref[...] = alpha * acc_ref[...] + pv
    m_ref[...] = m_new

    @pl.when(j == nsteps - 1)
    def _():
        o_ref[0] = (acc_ref[...] / l_ref[...]).astype(o_ref.dtype)


def _fox_sample(qm, kn, vn, f_row, bf_row, dp, ck4, cv4, layer, page_table):
    B, R, hd = qm.shape
    tn = R // H_C
    npg = page_table.shape[1]
    G = _pick(npg, (8, 4, 2, 1))
    nsteps = npg // G
    PW = PAGE * H_C
    seq = lambda b, j, pt: (b, 0, 0)
    page = lambda g: (lambda b, j, pt: (layer, pt[b, G * j + g], 0, 0))
    grid_spec = pltpu.PrefetchScalarGridSpec(
        num_scalar_prefetch=1,
        grid=(B, nsteps),
        in_specs=[pl.BlockSpec((1, R, hd), seq), pl.BlockSpec((1, R, hd), seq), pl.BlockSpec((1, R, hd), seq),
                  pl.BlockSpec((1, 1, LANES), seq),
                  pl.BlockSpec((1, LANES), lambda b, j, pt: (0, 0)),
                  pl.BlockSpec((1, G, 1, PW), lambda b, j, pt: (b, j, 0, 0))]
                 + [pl.BlockSpec((1, 1, PW, hd), page(g)) for g in range(G)] * 2,
        out_specs=[pl.BlockSpec((1, R, hd), seq), pl.BlockSpec((1, 1, LANES), seq)],
        scratch_shapes=[pltpu.VMEM((R, LANES), F32), pltpu.VMEM((R, LANES), F32),
                        pltpu.VMEM((R, hd), F32), pltpu.VMEM((R, LANES), F32)])
    return pl.pallas_call(
        functools.partial(_fox_sample_kernel, G=G, tn=tn, nsteps=nsteps, scale=hd ** -0.5),
        grid_spec=grid_spec,
        out_shape=[jax.ShapeDtypeStruct((B, R, hd), BF16), jax.ShapeDtypeStruct((B, 1, LANES), F32)],
        compiler_params=_cp(("parallel", "arbitrary")),
        name="fox_sample",
    )(page_table, qm, kn, vn, f_row, bf_row, dp, *([ck4] * G), *([cv4] * G))


def _gla_kernel(q_ref, k_ref, v_ref, g_ref, ld_ref, aw_ref, ab_ref, ng_ref, s0_ref, y_ref, sout_ref,
                s_ref, o_ref, u_s, sb_s, *, tt, nt, t_valid):
    j = pl.program_id(2)
    dk = q_ref.shape[-1]
    C = GLA_CHUNK
    nch = tt // C if t_valid is None else -(-t_valid // C)

    @pl.when(j == 0)
    def _():
        s_ref[...] = s0_ref[0, 0]

    z = jnp.dot(ld_ref[0], aw_ref[...], precision=HI, preferred_element_type=F32) + ab_ref[...]
    lga = _log_sigmoid(z) * (1.0 / GLA_TAU)
    kx = k_ref[0]
    if t_valid is not None:
        valid = _iota2((tt, dk), 0) < t_valid
        lga = jnp.where(valid, lga, 0.0)
        kx = jnp.where(valid, kx, 0.0)
    same = (_iota2((tt, tt), 0) // C) == (_iota2((tt, tt), 1) // C)
    upto = _iota2((tt, tt), 0) >= _iota2((tt, tt), 1)
    b = jnp.dot((same & upto).astype(F32), lga, precision=HI, preferred_element_type=F32)
    bend = jnp.dot(same.astype(F32), lga, precision=HI, preferred_element_type=F32)
    qx = q_ref[0] * (dk ** -0.5)
    qe = qx * jnp.exp(b)
    kdt = (kx * jnp.exp(bend - b)).T
    kdt_h = kdt.astype(BF16)
    kdt_l = (kdt - kdt_h.astype(F32)).astype(BF16)
    dect = jnp.exp(bend).T
    v = v_ref[0]
    v_h = v.astype(BF16)
    v_l = (v - v_h.astype(F32)).astype(BF16)
    zero = jnp.zeros_like(v_h)
    rowc = _iota2((tt, v.shape[-1]), 0) // C
    tri_rows = _iota2((C, dk), 0)
    for c in range(nch):
        vm_h = jnp.where(rowc == c, v_h, zero)
        vm_l = jnp.where(rowc == c, v_l, zero)
        u_s[c] = (jnp.dot(kdt_h, vm_h, preferred_element_type=F32)
                  + jnp.dot(kdt_l, vm_h, preferred_element_type=F32)
                  + jnp.dot(kdt_h, vm_l, preferred_element_type=F32))
    S = s_ref[...]
    for c in range(nch):
        sb_s[c] = S.astype(BF16)
        S = S * dect[:, C * c:C * c + 1] + u_s[c]
    s_ref[...] = S
    for c in range(nch):
        r0 = C * c
        q_c, k_c, b_c, v_c = qx[r0:r0 + C], kx[r0:r0 + C], b[r0:r0 + C], v[r0:r0 + C]
        o_c = jnp.dot(qe[r0:r0 + C].astype(BF16), sb_s[c], preferred_element_type=F32)
        for s in range(C):
            rel = jnp.exp(jnp.where(tri_rows >= s, b_c - b_c[s:s + 1, :], -jnp.inf))
            att = jnp.sum(q_c * k_c[s:s + 1, :] * rel, axis=1, keepdims=True)
            o_c = o_c + att * v_c[s:s + 1, :]
        o_ref[r0:r0 + C, :] = o_c
    if nch * C < tt:
        o_ref[nch * C:tt, :] = jnp.zeros((tt - nch * C, o_ref.shape[-1]), F32)
    od = o_ref[...]
    od = od * lax.rsqrt(jnp.mean(od * od, axis=-1, keepdims=True) + LN_EPS)
    gate = g_ref[0]
    y_ref[0] = (od * ng_ref[...] * (gate * jax.nn.sigmoid(gate))).astype(y_ref.dtype)

    @pl.when(j == nt - 1)
    def _():
        sout_ref[0, 0] = s_ref[...]


def _gla_mixer(u3, qc, kc, vc, gc, ldc, aw_pad, ab, ng, s0, t_valid=None):
    B, Tp, _ = u3.shape
    _, H, dk, dv = s0.shape
    tt = _pick(Tp, (256, 128))
    nt = Tp // tt if t_valid is None else 1
    return pl.pallas_call(
        functools.partial(_gla_kernel, tt=tt, nt=nt, t_valid=t_valid),
        grid=(B, H, nt),
        in_specs=[pl.BlockSpec((1, tt, dk), lambda b, h, j: (b, j, qc + h)),
                  pl.BlockSpec((1, tt, dk), lambda b, h, j: (b, j, kc + h)),
                  pl.BlockSpec((1, tt, dv), lambda b, h, j: (b, j, vc + h)),
                  pl.BlockSpec((1, tt, dv), lambda b, h, j: (b, j, gc + h)),
                  pl.BlockSpec((1, tt, LANES), lambda b, h, j: (b, j, ldc)),
                  pl.BlockSpec((LANES, dk), lambda b, h, j: (0, h)),
                  pl.BlockSpec((1, dk), lambda b, h, j: (0, h)),
                  pl.BlockSpec((1, dv), lambda b, h, j: (0, h)),
                  pl.BlockSpec((1, 1, dk, dv), lambda b, h, j: (b, h, 0, 0))],
        out_specs=[pl.BlockSpec((1, tt, dv), lambda b, h, j: (b, j, h)),
                   pl.BlockSpec((1, 1, dk, dv), lambda b, h, j: (b, h, 0, 0))],
        out_shape=[jax.ShapeDtypeStruct((B, nt * tt, H * dv), BF16),
                   jax.ShapeDtypeStruct((B, H, dk, dv), F32)],
        scratch_shapes=[pltpu.VMEM((dk, dv), F32), pltpu.VMEM((tt, dv), F32),
                        pltpu.VMEM((tt // GLA_CHUNK, dk, dv), F32), pltpu.VMEM((tt // GLA_CHUNK, dk, dv), BF16)],
        compiler_params=_cp(("parallel", "parallel", "arbitrary")),
        name="gla",
    )(u3, u3, u3, u3, u3, aw_pad, ab.reshape(1, -1), ng.reshape(1, -1), s0)


def _pack_w_in(w, fc, wd, dkq):
    D = w.shape[0]
    d0 = fc + H_C
    ld0 = d0 + 2 * dkq + 2 * wd
    n_ld = w.shape[1] - ld0
    return jnp.concatenate([w[:, :fc], w[:, d0:ld0], w[:, fc:d0], jnp.zeros((D, LANES - H_C), w.dtype),
                            w[:, ld0:], jnp.zeros((D, LANES - n_ld), w.dtype)], axis=1)


def _token_mixing(xb, B, T, P, layer, st, sample):
    M, D = xb.shape
    wa = wb = wc = D // 4
    wd = D - 3 * (D // 4)
    dkq = wd // 2
    hd = wc // H_C
    n_ld = P['gla_aw'].shape[0]
    cB = 2 * wa
    cC = cB + 3 * wb + 2 * LANES
    cD = cC + 3 * wc
    tm = _pick(M, (1024, 32))
    w_in = _pack_w_in(P['w_in'], cD, wd, dkq)
    N = w_in.shape[1]
    u = _matmul(xb, w_in, (), tm, _pick(N, (512,)), D)
    u3 = u.reshape(B, T, N)
    cF = cD + 2 * dkq + 2 * wd
    cL = cF + LANES
    bf_pad = jnp.concatenate([P['fox_bf'], jnp.zeros((LANES - H_C,), F32)])
    aw_pad = jnp.concatenate([P['gla_aw'], jnp.zeros((LANES - n_ld, dkq), F32)], axis=0)
    kc = u3[:, :, cC + wc:cC + 2 * wc].reshape(B, T, H_C, hd)
    vc = u3[:, :, cC + 2 * wc:cC + 3 * wc].reshape(B, T, H_C, hd)
    shift_new = u3[:, T - 1, cB:cB + 3 * wb + 2 * LANES]

    ya, conv_new = _conv_mixer(u3, st['conv'], P['conv_w'], P['conv_b'], P['conv_ln_g'], P['conv_ln_b'])

    if sample is None:
        up, t_valid = u3, None
    else:
        up, t_valid = jnp.pad(u3, ((0, 0), (0, RW_TILE - T), (0, 0))), T
    yb, wkv_new = _rwkv_mixer(up, cB // wb, st['shift'], st['wkv'], P, t_valid)
    yd, gla_new = _gla_mixer(up, cD // (dkq // H_D), (cD + dkq) // (dkq // H_D),
                             (cD + 2 * dkq) // (wd // H_D), (cD + 2 * dkq + wd) // (wd // H_D),
                             cL // LANES, aw_pad, P['gla_ab'], P['gla_ng'], st['gla'], t_valid)
    yb, yd = yb[:, :T], yd[:, :T]

    if sample is None:
        logf_pad, cq, ck = _fox_prep(u3, cF // LANES, bf_pad.reshape(1, LANES))
        logf = logf_pad[:, :, :H_C]
        yc = _fox_prompt(u3, cC // hd, cq, ck, hd)
    else:
        R = T * H_C
        rows = lambda c0: u3[:, :, c0:c0 + wc].reshape(B, R, hd)
        f_row = jnp.pad(u3[:, :, cF:cF + H_C].reshape(B, 1, R), ((0, 0), (0, 0), (0, LANES - R)))
        bf_row = jnp.pad(jnp.tile(P['fox_bf'], T), (0, LANES - R)).reshape(1, LANES)
        dpt = _fox_decay(sample['clf_t'], layer, sample['page_table'])
        dp = dpt.transpose(0, 1, 3, 2).reshape(B, dpt.shape[1], 1, PAGE * H_C)
        yc, lfo = _fox_sample(rows(cC), rows(cC + wc), rows(cC + 2 * wc), f_row, bf_row, dp,
                              sample['ck'], sample['cv'], layer, sample['page_table'])
        yc = yc.reshape(B, T, wc)
        logf = lfo[:, 0, :R].reshape(B, T, H_C)

    ymix = jnp.concatenate([ya, yb, yc, yd], axis=-1).reshape(M, D)
    y = _matmul(ymix, P['w_o_all'], (layer,), tm, _pick(D, (512,)), D)
    return y, (kc, vc, logf, conv_new, shift_new, wkv_new, gla_new)


def _swiglu(xb, P, layer, which):
    M, D = xb.shape
    F = P['ffn_wg_all'].shape[-1]
    widx = (layer, which)
    h = _ffn_up(xb, P['ffn_wg_all'], P['ffn_wu_all'], widx, _pick(M, (1024, 32)), _pick(F, (256,)))
    return _matmul(h, P['ffn_wd_all'], widx, _pick(M, (512, 32)), _pick(D, (256,)), F)


def _layer(x, xb, B, T, P, layer, st, sample, alpha):
    x, xb = _add_ln(x, _swiglu(xb, P, layer, 0), P['ln_g'][0], P['ln_b'][0], alpha, 0.5)
    y, new = _token_mixing(xb, B, T, P, layer, st, sample)
    x, xb = _add_ln(x, y, P['ln_g'][1], P['ln_b'][1], alpha, 1.0)
    x, xb = _add_ln(x, _swiglu(xb, P, layer, 1), P['ln_g'][2], P['ln_b'][2], alpha, 0.5)
    return x, xb, new


def kernel(x_prompt, x_sample, cache_k, cache_v, cache_logf, state_conv, state_shift, state_wkv, state_gla, page_table, ln_g, ln_b, ffn_wg, ffn_wu, ffn_wd, w_in, w_o, conv_w, conv_b, conv_ln_g, conv_ln_b, rwkv_mu, rwkv_w0, rwkv_w2, rwkv_a0, rwkv_a2, rwkv_g2, rwkv_kk, rwkv_ka, rwkv_rk, rwkv_gn_g, rwkv_gn_b, fox_bf, gla_aw, gla_ab, gla_ng):
    big = dict(ffn_wg_all=ffn_wg, ffn_wu_all=ffn_wu, ffn_wd_all=ffn_wd.astype(BF16), w_o_all=w_o)
    stacked = dict(ln_g=ln_g, ln_b=ln_b, w_in=w_in, conv_w=conv_w, conv_b=conv_b, conv_ln_g=conv_ln_g,
                   conv_ln_b=conv_ln_b, rwkv_mu=rwkv_mu, rwkv_w0=rwkv_w0, rwkv_w2=rwkv_w2, rwkv_a0=rwkv_a0,
                   rwkv_a2=rwkv_a2, rwkv_g2=rwkv_g2, rwkv_kk=rwkv_kk, rwkv_ka=rwkv_ka, rwkv_rk=rwkv_rk,
                   rwkv_gn_g=rwkv_gn_g, rwkv_gn_b=rwkv_gn_b, fox_bf=fox_bf, gla_aw=gla_aw, gla_ab=gla_ab,
                   gla_ng=gla_ng)
    depth = w_in.shape[0]
    Bp, Tp, D = x_prompt.shape
    Bs, Ts, _ = x_sample.shape
    alpha = (2.0 * depth) ** 0.25
    n_pool, page, hc, hdc = cache_k.shape[1:]
    ck = cache_k.reshape(depth, n_pool, page * hc, hdc)
    cv = cache_v.reshape(depth, n_pool, page * hc, hdc)
    clf_t = cache_logf.transpose(0, 1, 3, 2)
    xp, xs = x_prompt.reshape(Bp * Tp, D), x_sample.reshape(Bs * Ts, D)
    xpb, xsb = xp.astype(BF16), xs.astype(BF16)
    zero_state = dict(conv=jnp.zeros((Bp,) + state_conv.shape[2:], F32),
                      shift=jnp.zeros((Bp,) + state_shift.shape[2:], F32),
                      wkv=jnp.zeros((Bp,) + state_wkv.shape[2:], F32),
                      gla=jnp.zeros((Bp,) + state_gla.shape[2:], F32))
    st_p, st_s = [], []
    for l in range(depth):
        P = {n: a[l] for n, a in stacked.items()}
        P.update(big)
        xp, xpb, s = _layer(xp, xpb, Bp, Tp, P, l, zero_state, None, alpha)
        st_p.append(s)
        st = dict(conv=state_conv[l], shift=state_shift[l], wkv=state_wkv[l], gla=state_gla[l])
        xs, xsb, s = _layer(xs, xsb, Bs, Ts, P, l, st,
                            dict(ck=ck, cv=cv, clf_t=clf_t, page_table=page_table), alpha)
        st_s.append(s)
    stk = lambda sts, i: jnp.stack([s[i] for s in sts])
    return ((xp.reshape(Bp, Tp, D), xs.reshape(Bs, Ts, D))
            + tuple(stk(st_p, i) for i in range(7)) + tuple(stk(st_s, i) for i in range(7)))
```

```python
import functools

import numpy as np
import jax
import jax.numpy as jnp
from jax import lax
from jax.experimental import pallas as pl
from jax.experimental.pallas import tpu as pltpu

F32 = jnp.float32
BF16 = jnp.bfloat16
HI = lax.Precision.HIGHEST

LANES = 128
VMEM_LIMIT = 56 * 1024 * 1024

LN_EPS = 1e-5
GN_EPS = 64e-5
CONV_K = 31
HD_B = 64
H_C = 8
H_D = 4
GLA_TAU = 16.0
GLA_CHUNK = 16
PAGE = 128
RW_TILE = 128
RW_SUB = 16


def _cp(sem):
    return pltpu.CompilerParams(dimension_semantics=sem, vmem_limit_bytes=VMEM_LIMIT)


def _pick(n, prefs):
    for p in prefs:
        if n % p == 0:
            return p
    return n


def _log_sigmoid(x):
    return jnp.minimum(x, 0.0) - jnp.log1p(jnp.exp(-jnp.abs(x)))


def _softplus(x):
    return jnp.maximum(x, 0.0) + jnp.log1p(jnp.exp(-jnp.abs(x)))


def _split_dot(x, w_bf16, n):
    acc = None
    rem = x
    for i in range(n):
        h = rem.astype(BF16)
        d = jnp.dot(h, w_bf16, preferred_element_type=F32)
        acc = d if acc is None else acc + d
        if i < n - 1:
            rem = rem - h.astype(F32)
    return acc


def _iota2(shape, dim):
    return lax.broadcasted_iota(jnp.int32, shape, dim)


def _mm_kernel(x_ref, w_ref, o_ref, acc_ref, *, nk):
    part = jnp.dot(x_ref[...].astype(BF16), w_ref[...].astype(BF16), preferred_element_type=F32)
    if nk == 1:
        o_ref[...] = part.astype(o_ref.dtype)
        return
    k = pl.program_id(2)

    @pl.when(k == 0)
    def _():
        acc_ref[...] = part

    @pl.when(k > 0)
    def _():
        acc_ref[...] += part

    @pl.when(k == nk - 1)
    def _():
        o_ref[...] = acc_ref[...].astype(o_ref.dtype)


def _matmul(x, w, widx, tm, tn, tk, out_dtype=F32):
    M, K = x.shape
    N = w.shape[-1]
    nk = K // tk
    return pl.pallas_call(
        functools.partial(_mm_kernel, nk=nk),
        grid=(M // tm, N // tn, nk),
        in_specs=[pl.BlockSpec((tm, tk), lambda i, j, k: (i, k)),
                  pl.BlockSpec((None,) * len(widx) + (tk, tn), lambda i, j, k: widx + (k, j))],
        out_specs=pl.BlockSpec((tm, tn), lambda i, j, k: (i, j)),
        out_shape=jax.ShapeDtypeStruct((M, N), out_dtype),
        scratch_shapes=[pltpu.VMEM((tm, tn) if nk > 1 else (8, LANES), F32)],
        compiler_params=_cp(("parallel", "parallel", "arbitrary")),
        name="matmul",
    )(x, w)


def _ffn_up_kernel(x_ref, wg_ref, wu_ref, o_ref):
    x = x_ref[...]
    g = jnp.dot(x, wg_ref[...].astype(BF16), preferred_element_type=F32)
    u = jnp.dot(x, wu_ref[...].astype(BF16), preferred_element_type=F32)
    o_ref[...] = (g * jax.nn.sigmoid(g) * u).astype(o_ref.dtype)


def _ffn_up(xb, wg, wu, widx, tm, tn):
    M, K = xb.shape
    N = wg.shape[-1]
    wspec = pl.BlockSpec((None,) * len(widx) + (K, tn), lambda i, j: widx + (0, j))
    return pl.pallas_call(
        _ffn_up_kernel,
        grid=(M // tm, N // tn),
        in_specs=[pl.BlockSpec((tm, K), lambda i, j: (i, 0)), wspec, wspec],
        out_specs=pl.BlockSpec((tm, tn), lambda i, j: (i, j)),
        out_shape=jax.ShapeDtypeStruct((M, N), BF16),
        compiler_params=_cp(("parallel", "arbitrary")),
        name="ffn_up",
    )(xb, wg, wu)


def _add_ln_kernel(x_ref, y_ref, g_ref, b_ref, of_ref, ob_ref, *, alpha, c):
    z = alpha * x_ref[...] + c * y_ref[...]
    mu = jnp.mean(z, axis=-1, keepdims=True)
    zc = z - mu
    var = jnp.mean(zc * zc, axis=-1, keepdims=True)
    o = zc * lax.rsqrt(var + LN_EPS) * g_ref[...] + b_ref[...]
    of_ref[...] = o
    ob_ref[...] = o.astype(BF16)


def _add_ln(x, y, g, b, alpha, c):
    M, D = x.shape
    tr = _pick(M, (256, 32))
    return pl.pallas_call(
        functools.partial(_add_ln_kernel, alpha=alpha, c=c),
        grid=(M // tr,),
        in_specs=[pl.BlockSpec((tr, D), lambda i: (i, 0)),
                  pl.BlockSpec((tr, D), lambda i: (i, 0)),
                  pl.BlockSpec((1, D), lambda i: (0, 0)),
                  pl.BlockSpec((1, D), lambda i: (0, 0))],
        out_specs=[pl.BlockSpec((tr, D), lambda i: (i, 0)),
                   pl.BlockSpec((tr, D), lambda i: (i, 0))],
        out_shape=[jax.ShapeDtypeStruct((M, D), F32), jax.ShapeDtypeStruct((M, D), BF16)],
        compiler_params=_cp(("parallel",)),
        name="add_ln",
    )(x, y, g.reshape(1, D), b.reshape(1, D))


def _conv_kernel(val_ref, gate_ref, buf_ref, w_ref, cb_ref, g_ref, b_ref, y_ref, new_ref, zz_ref,
                 *, tt, nt):
    j = pl.program_id(1)
    hist = CONV_K - 1

    @pl.when(j == 0)
    def _():
        zz_ref[2:2 + hist, :] = buf_ref[0]

    zz_ref[32:32 + tt, :] = val_ref[0] * jax.nn.sigmoid(gate_ref[0])
    rc = min(tt, 32)
    for r0 in range(0, tt, rc):
        acc = None
        for k in range(CONV_K):
            term = w_ref[k:k + 1, :] * zz_ref[2 + k + r0:2 + k + r0 + rc, :]
            acc = term if acc is None else acc + term
        y = acc + cb_ref[...]
        mu = jnp.mean(y, axis=-1, keepdims=True)
        yc = y - mu
        var = jnp.mean(yc * yc, axis=-1, keepdims=True)
        yn = yc * lax.rsqrt(var + LN_EPS) * g_ref[...] + b_ref[...]
        y_ref[0, r0:r0 + rc, :] = (yn * jax.nn.sigmoid(yn)).astype(y_ref.dtype)
    tail = zz_ref[2 + tt:2 + tt + hist, :]
    zz_ref[2:2 + hist, :] = tail

    @pl.when(j == nt - 1)
    def _():
        new_ref[0] = tail


def _conv_mixer(u3, buf, cw, cb, lg, lb):
    B, T, _ = u3.shape
    W = buf.shape[-1]
    tt = _pick(T, (256,))
    nt = T // tt
    return pl.pallas_call(
        functools.partial(_conv_kernel, tt=tt, nt=nt),
        grid=(B, nt),
        in_specs=[pl.BlockSpec((1, tt, W), lambda b, j: (b, j, 0)),
                  pl.BlockSpec((1, tt, W), lambda b, j: (b, j, 1)),
                  pl.BlockSpec((1, CONV_K - 1, W), lambda b, j: (b, 0, 0)),
                  pl.BlockSpec((CONV_K, W), lambda b, j: (0, 0)),
                  pl.BlockSpec((1, W), lambda b, j: (0, 0)),
                  pl.BlockSpec((1, W), lambda b, j: (0, 0)),
                  pl.BlockSpec((1, W), lambda b, j: (0, 0))],
        out_specs=[pl.BlockSpec((1, tt, W), lambda b, j: (b, j, 0)),
                   pl.BlockSpec((1, CONV_K - 1, W), lambda b, j: (b, 0, 0))],
        out_shape=[jax.ShapeDtypeStruct((B, T, W), BF16),
                   jax.ShapeDtypeStruct((B, CONV_K - 1, W), F32)],
        scratch_shapes=[pltpu.VMEM((32 + tt, W), F32)],
        compiler_params=_cp(("parallel", "arbitrary")),
        name="conv_mixer",
    )(u3, u3, buf, cw, cb.reshape(1, W), lg.reshape(1, W), lb.reshape(1, W))


def _rwkv_kernel(r_ref, k_ref, v_ref, lwa_ref, lg_ref, shift_ref, s0_ref, mu_ref, w0_ref, w2h_ref, w2l_ref,
                 a0_ref, a2h_ref, a2l_ref, g2_ref, kkp_ref, ka_ref, rk_ref, gng_ref, gnb_ref, sel_ref,
                 y_ref, sout_ref,
                 s_ref, prev_ref, kk_s, d_s, bv_s, k2_s, rp_s, v_s, rho_s, rks_s, g_s, lhh_s, lhl_s,
                 vbuf_s, ybuf_s, *, G, nt, t_valid):
    j = pl.program_id(1)
    TT = RW_TILE
    WB = r_ref.shape[-1]
    npair = WB // LANES

    @pl.when(j == 0)
    def _():
        s_ref[...] = s0_ref[...]
        prev_ref[...] = shift_ref[...]

    ob = (_iota2((LANES, LANES), 0) // HD_B == _iota2((LANES, LANES), 1) // HD_B).astype(BF16)
    obd = (_iota2((2 * LANES, 2 * LANES), 0) // HD_B == _iota2((2 * LANES, 2 * LANES), 1) // HD_B).astype(BF16)

    def segsum(x):
        rows = x.shape[0]
        tall = jnp.concatenate([x[:, LANES * p:LANES * (p + 1)] for p in range(npair)], axis=0)
        s = _split_dot(tall, ob, 2)
        return jnp.concatenate([s[rows * p:rows * (p + 1)] for p in range(npair)], axis=1)

    def dot3(x, wh_ref, wl_ref):
        xh = x.astype(BF16)
        xl = (x - xh.astype(F32)).astype(BF16)
        wh = wh_ref[...]
        return (jnp.dot(xh, wh, preferred_element_type=F32) + jnp.dot(xl, wh, preferred_element_type=F32)
                + jnp.dot(xh, wl_ref[...], preferred_element_type=F32))

    row0 = _iota2((TT, 1), 0) == 0
    last = (t_valid if t_valid is not None else TT) - 1
    for gi in range(G):
        def shifted(x, lo, hi, gi=gi):
            prev = jnp.where(row0, prev_ref[gi, :, lo:hi], pltpu.roll(x, 1, 0))
            return x + (prev - x) * mu_ref[:, lo:hi]

        raw = (r_ref[gi], k_ref[gi], v_ref[gi], lwa_ref[gi], lg_ref[gi])
        offs = (0, WB, 2 * WB, 3 * WB, 3 * WB + LANES, 3 * WB + 2 * LANES)
        r, k, v, lwa, lg = [shifted(x, offs[i], offs[i + 1]) for i, x in enumerate(raw)]
        for i, x in enumerate(raw):
            prev_ref[gi, :, offs[i]:offs[i + 1]] = x[last:last + 1]

        w = -_softplus(-(w0_ref[...] + dot3(jnp.tanh(lwa), w2h_ref, w2l_ref))) - 0.5
        d = jnp.exp(-jnp.exp(w))
        a = jax.nn.sigmoid(a0_ref[...] + dot3(lwa, a2h_ref, a2l_ref))
        g = jnp.dot(jax.nn.sigmoid(lg).astype(BF16), g2_ref[...], preferred_element_type=F32)
        kk = k * kkp_ref[...]
        kk = kk * lax.rsqrt(jnp.maximum(segsum(kk * kk), 1e-24))
        k2 = k * (1.0 + (a - 1.0) * ka_ref[...])
        bv = kk * a
        kk_s[gi] = kk
        d_s[gi] = d
        bv_s[gi] = bv
        k2_s[gi] = k2
        rp_s[gi] = d * r - kk * segsum(bv * r)
        v_s[gi] = v
        rho_s[gi] = segsum(k2 * r)
        rks_s[gi] = segsum(r * k2 * rk_ref[...])
        g_s[gi] = g

        vt = v.T
        vh = vt.astype(BF16).astype(F32)
        vl = vt - vh
        for p in range(npair):
            for h in range(2):
                src = slice(LANES * p + HD_B * h, LANES * p + HD_B * (h + 1))
                lhh_s[gi, HD_B * p:HD_B * (p + 1), TT * h:TT * (h + 1)] = vh[src, :]
                lhl_s[gi, HD_B * p:HD_B * (p + 1), TT * h:TT * (h + 1)] = vl[src, :]

    ybuf_s[...] = jnp.zeros_like(ybuf_s)
    lane_x = _iota2((npair * HD_B, LANES), 1)
    lane_c = lane_x % HD_B
    n_steps = TT if t_valid is None else t_valid
    n_sub = -(-n_steps // RW_SUB)
    for sub in range(n_sub):
        base = (TT - RW_SUB * sub) % TT
        for gi in range(G):
            parts = [lhh_s[gi, :, 0:TT], lhh_s[gi, :, TT:2 * TT], lhl_s[gi, :, 0:TT], lhl_s[gi, :, TT:2 * TT]]
            x = 0.0
            for q in range(3, -1, -1):
                sh = (base + RW_SUB * q) % TT
                part = pltpu.roll(parts[q], sh, 1) if sh else parts[q]
                x = jnp.where(lane_x < RW_SUB * (q + 1), part, x)
            xb = x.astype(BF16)
            for t2 in range(RW_SUB // 2):
                res = jnp.dot(xb, sel_ref[:, 2 * LANES * t2:2 * LANES * (t2 + 1)], preferred_element_type=F32)
                vbuf_s[gi, 2 * t2] = res[:, 0:LANES]
                vbuf_s[gi, 2 * t2 + 1] = res[:, LANES:2 * LANES]

        def step(tl, carry, sub=sub):
            t = RW_SUB * sub + tl
            half = t // HD_B
            mask = lane_c == (t % HD_B)

            def tile(ref, gi):
                row = ref[gi, pl.ds(t, 1), :]
                return jnp.concatenate(
                    [jnp.broadcast_to(row[:, LANES * p:LANES * (p + 1)], (HD_B, LANES)) for p in range(npair)],
                    axis=0)

            for gi in range(G):
                S = s_ref[gi]
                lhs = jnp.concatenate([(S * tile(kk_s, gi)).astype(BF16), (S * tile(rp_s, gi)).astype(BF16)],
                                      axis=1)
                res = jnp.dot(lhs, obd, preferred_element_type=F32)
                s_ref[gi] = (S * tile(d_s, gi) - res[:, 0:LANES] * tile(bv_s, gi)
                             + vbuf_s[gi, tl] * tile(k2_s, gi))
                ybuf_s[gi, half] = jnp.where(mask, res[:, LANES:2 * LANES], ybuf_s[gi, half])
            return carry

        lax.fori_loop(0, min(RW_SUB, n_steps - RW_SUB * sub), step, 0, unroll=4)

    for gi in range(G):
        halves = []
        for hf in range(TT // HD_B):
            yt = ybuf_s[gi, hf].T
            cols = []
            for p in range(npair):
                cols.append(yt[0:HD_B, HD_B * p:HD_B * (p + 1)])
                cols.append(yt[HD_B:2 * HD_B, HD_B * p:HD_B * (p + 1)])
            halves.append(jnp.concatenate(cols, axis=1))
        v = v_s[gi]
        y = jnp.concatenate(halves, axis=0) + v * rho_s[gi]
        m = segsum(y) * (1.0 / HD_B)
        yc = y - m
        var = segsum(yc * yc) * (1.0 / HD_B)
        yn = yc * lax.rsqrt(var + GN_EPS) * gng_ref[...] + gnb_ref[...]
        y_ref[gi] = ((yn + rks_s[gi] * v) * g_s[gi]).astype(y_ref.dtype)

    @pl.when(j == nt - 1)
    def _():
        sout_ref[...] = s_ref[...]


def _rwkv_sel():
    rows = np.arange(LANES)
    cols = np.arange(RW_SUB * LANES)
    hit = (rows[:, None] < 4 * RW_SUB) & \
          (((rows[:, None] % (2 * RW_SUB)) // RW_SUB) == ((cols[None, :] % LANES) // HD_B)) & \
          ((rows[:, None] % RW_SUB) == (cols[None, :] // LANES))
    return jnp.asarray(hit, dtype=BF16)


def _rwkv_mixer(u3, cb, shift, wkv, P, t_valid=None):
    B, Tp, _ = u3.shape
    WB = P['rwkv_w0'].shape[-1]
    H = WB // HD_B
    npair = H // 2
    G = 2 if B % 2 == 0 else 1
    nt = Tp // RW_TILE if t_valid is None else 1
    NB = 3 * WB + 2 * LANES
    lb = (cb + 3) * (WB // LANES)
    s0 = wkv.reshape(B, npair, 2, HD_B, HD_B).transpose(0, 1, 3, 2, 4).reshape(B, npair * HD_B, LANES)
    zpad = jnp.zeros((HD_B, WB), F32)
    w2 = jnp.concatenate([P['rwkv_w2'], zpad], axis=0)
    a2 = jnp.concatenate([zpad, P['rwkv_a2']], axis=0)
    hi = lambda w: w.astype(BF16)
    lo = lambda w: (w - w.astype(BF16).astype(F32)).astype(BF16)
    row = lambda x: x.reshape(1, -1)
    full = lambda shape: pl.BlockSpec(shape, lambda b, j: (0,) * len(shape))
    tok = lambda w, c: pl.BlockSpec((G, RW_TILE, w), lambda b, j, c=c: (b, j, c))
    SR = npair * HD_B
    y, snew = pl.pallas_call(
        functools.partial(_rwkv_kernel, G=G, nt=nt, t_valid=t_valid),
        grid=(B // G, nt),
        in_specs=[tok(WB, cb), tok(WB, cb + 1), tok(WB, cb + 2), tok(LANES, lb), tok(LANES, lb + 1),
                  pl.BlockSpec((G, 1, NB), lambda b, j: (b, 0, 0)),
                  pl.BlockSpec((G, SR, LANES), lambda b, j: (b, 0, 0)),
                  full((1, NB)), full((1, WB)), full((LANES, WB)), full((LANES, WB)), full((1, WB)),
                  full((LANES, WB)), full((LANES, WB)), full((LANES, WB)), full((1, WB)), full((1, WB)),
                  full((1, WB)), full((1, WB)), full((1, WB)), full((LANES, RW_SUB * LANES))],
        out_specs=[pl.BlockSpec((G, RW_TILE, WB), lambda b, j: (b, j, 0)),
                   pl.BlockSpec((G, SR, LANES), lambda b, j: (b, 0, 0))],
        out_shape=[jax.ShapeDtypeStruct((B, nt * RW_TILE, WB), BF16),
                   jax.ShapeDtypeStruct((B, SR, LANES), F32)],
        scratch_shapes=[pltpu.VMEM((G, SR, LANES), F32),
                        pltpu.VMEM((G, 1, NB), F32)]
                       + [pltpu.VMEM((G, RW_TILE, WB), F32)] * 9
                       + [pltpu.VMEM((G, SR, 2 * RW_TILE), F32)] * 2
                       + [pltpu.VMEM((G, RW_SUB, SR, LANES), F32),
                          pltpu.VMEM((G, RW_TILE // HD_B, SR, LANES), F32)],
        compiler_params=_cp(("parallel", "arbitrary")),
        name="rwkv7",
    )(u3, u3, u3, u3, u3, shift.reshape(B, 1, NB), s0, row(P['rwkv_mu']), row(P['rwkv_w0']), hi(w2), lo(w2),
      row(P['rwkv_a0']), hi(a2), lo(a2), hi(P['rwkv_g2']), row(P['rwkv_kk']), row(P['rwkv_ka']),
      row(P['rwkv_rk']), row(P['rwkv_gn_g']), row(P['rwkv_gn_b']), _rwkv_sel())
    snew = snew.reshape(B, npair, HD_B, 2, HD_B).transpose(0, 1, 3, 2, 4).reshape(B, H, HD_B, HD_B)
    return y, snew


def _fox_prep_kernel(f_ref, bf_ref, logf_ref, cq_ref, ck_ref, carry_ref, *, tc):
    j = pl.program_id(1)

    @pl.when(j == 0)
    def _():
        carry_ref[...] = jnp.zeros_like(carry_ref)

    lf = _log_sigmoid(f_ref[0] + bf_ref[...])
    logf_ref[0] = lf
    tri = (_iota2((tc, tc), 0) >= _iota2((tc, tc), 1)).astype(F32)
    c = jnp.dot(tri, lf, precision=HI, preferred_element_type=F32) + carry_ref[...]
    carry_ref[...] = c[tc - 1:tc, :]
    ct = c.T
    for h in range(H_C):
        cq_ref[0, h] = jnp.broadcast_to(c[:, h:h + 1], (tc, LANES))
        ck_ref[0, h] = ct[h:h + 1, :]


def _fox_prep(u3, fcol, bf_pad):
    B, T, _ = u3.shape
    tc = _pick(T, (256,))
    return pl.pallas_call(
        functools.partial(_fox_prep_kernel, tc=tc),
        grid=(B, T // tc),
        in_specs=[pl.BlockSpec((1, tc, LANES), lambda b, j: (b, j, fcol)),
                  pl.BlockSpec((1, LANES), lambda b, j: (0, 0))],
        out_specs=[pl.BlockSpec((1, tc, LANES), lambda b, j: (b, j, 0)),
                   pl.BlockSpec((1, H_C, tc, LANES), lambda b, j: (b, 0, j, 0)),
                   pl.BlockSpec((1, H_C, 1, tc), lambda b, j: (b, 0, 0, j))],
        out_shape=[jax.ShapeDtypeStruct((B, T, LANES), F32),
                   jax.ShapeDtypeStruct((B, H_C, T, LANES), F32),
                   jax.ShapeDtypeStruct((B, H_C, 1, T), F32)],
        scratch_shapes=[pltpu.VMEM((1, LANES), F32)],
        compiler_params=_cp(("parallel", "arbitrary")),
        name="fox_prep",
    )(u3, bf_pad)


def _fox_flash_kernel(q_ref, k_ref, v_ref, cq_ref, ck_ref, o_ref, m_ref, l_ref, acc_ref,
                      *, tq, tk, nk, scale):
    i = pl.program_id(2)
    j = pl.program_id(3)

    @pl.when(j == 0)
    def _():
        m_ref[...] = jnp.full_like(m_ref, -jnp.inf)
        l_ref[...] = jnp.zeros_like(l_ref)
        acc_ref[...] = jnp.zeros_like(acc_ref)

    @pl.when(j * tk <= i * tq + tq - 1)
    def _():
        s = lax.dot_general(q_ref[0].astype(BF16), k_ref[0].astype(BF16), (((1,), (1,)), ((), ())),
                            preferred_element_type=F32) * scale
        rep = tk // LANES
        s = s + jnp.concatenate([cq_ref[0, 0]] * rep, axis=1) - ck_ref[0, 0]
        qpos = i * tq + _iota2((tq, tk), 0)
        kpos = j * tk + _iota2((tq, tk), 1)
        s = jnp.where(kpos <= qpos, s, -jnp.inf)
        m_prev = m_ref[...]
        m_new = jnp.maximum(m_prev, jnp.max(s, axis=1, keepdims=True))
        p = jnp.exp(s - jnp.concatenate([m_new] * rep, axis=1))
        alpha = jnp.exp(m_prev - m_new)
        l_ref[...] = alpha * l_ref[...] + jnp.sum(p, axis=1, keepdims=True)
        acc_ref[...] = alpha * acc_ref[...] + jnp.dot(p.astype(BF16), v_ref[0].astype(BF16),
                                                      preferred_element_type=F32)
        m_ref[...] = m_new

    @pl.when(j == nk - 1)
    def _():
        o_ref[0] = (acc_ref[...] / l_ref[...]).astype(o_ref.dtype)


def _fox_prompt(u3, qcol, cq, ck, hd):
    B, T, _ = u3.shape
    tq = tk = _pick(T, (512, 256))
    nq, nk = T // tq, T // tk
    kmap = lambda off: (lambda b, h, i, j: (b, jnp.minimum(j, (i * tq + tq - 1) // tk), off + h))
    return pl.pallas_call(
        functools.partial(_fox_flash_kernel, tq=tq, tk=tk, nk=nk, scale=hd ** -0.5),
        grid=(B, H_C, nq, nk),
        in_specs=[pl.BlockSpec((1, tq, hd), lambda b, h, i, j: (b, i, qcol + h)),
                  pl.BlockSpec((1, tk, hd), kmap(qcol + H_C)),
                  pl.BlockSpec((1, tk, hd), kmap(qcol + 2 * H_C)),
                  pl.BlockSpec((1, 1, tq, LANES), lambda b, h, i, j: (b, h, i, 0)),
                  pl.BlockSpec((1, 1, 1, tk),
                               lambda b, h, i, j: (b, h, 0, jnp.minimum(j, (i * tq + tq - 1) // tk)))],
        out_specs=pl.BlockSpec((1, tq, hd), lambda b, h, i, j: (b, i, h)),
        out_shape=jax.ShapeDtypeStruct((B, T, H_C * hd), BF16),
        scratch_shapes=[pltpu.VMEM((tq, LANES), F32), pltpu.VMEM((tq, LANES), F32),
                        pltpu.VMEM((tq, hd), F32)],
        compiler_params=_cp(("parallel", "parallel", "parallel", "arbitrary")),
        name="fox_prompt",
    )(u3, u3, u3, cq, ck)


def _fox_decay_kernel(pt_ref, *refs, nb):
    clf_refs, d_ref, tail_ref = refs[:nb], refs[nb], refs[nb + 1]
    j = pl.program_id(0)

    @pl.when(j == 0)
    def _():
        tail_ref[...] = jnp.zeros_like(tail_ref)

    lf = jnp.concatenate([r[0, 0] for r in clf_refs], axis=0)
    later = (_iota2((PAGE, PAGE), 0) > _iota2((PAGE, PAGE), 1)).astype(F32)
    d = jnp.dot(lf, later, precision=HI, preferred_element_type=F32) + tail_ref[...]
    tail_ref[...] = tail_ref[...] + jnp.dot(lf, jnp.ones((PAGE, PAGE), F32), precision=HI,
                                            preferred_element_type=F32)
    for b in range(nb):
        d_ref[b, 0] = d[H_C * b:H_C * (b + 1)]


def _fox_decay(clf_t, layer, page_table):
    B, npg = page_table.shape
    page = lambda b: (lambda j, pt: (layer, pt[b, npg - 1 - j], 0, 0))
    grid_spec = pltpu.PrefetchScalarGridSpec(
        num_scalar_prefetch=1,
        grid=(npg,),
        in_specs=[pl.BlockSpec((1, 1, H_C, PAGE), page(b)) for b in range(B)],
        out_specs=pl.BlockSpec((B, 1, H_C, PAGE), lambda j, pt: (0, npg - 1 - j, 0, 0)),
        scratch_shapes=[pltpu.VMEM((B * H_C, PAGE), F32)])
    return pl.pallas_call(
        functools.partial(_fox_decay_kernel, nb=B),
        grid_spec=grid_spec,
        out_shape=jax.ShapeDtypeStruct((B, npg, H_C, PAGE), F32),
        compiler_params=_cp(("arbitrary",)),
        name="fox_decay",
    )(page_table, *([clf_t] * B))


def _fox_sample_kernel(pt_ref, q_ref, kn_ref, vn_ref, f_ref, bf_ref, dp_ref, *refs, G, tn, nsteps, scale):
    k_refs, v_refs = refs[:G], refs[G:2 * G]
    o_ref, lfo_ref, m_ref, l_ref, acc_ref, cnq_ref = refs[2 * G:]
    j = pl.program_id(1)
    R = tn * H_C
    hd = q_ref.shape[-1]
    PW = PAGE * H_C
    rep = lambda x, n: jnp.concatenate([x] * n, axis=1)

    @pl.when(j == 0)
    def _():
        lane = _iota2((H_C, LANES), 1)
        lf = jnp.where(lane < R, _log_sigmoid(jnp.broadcast_to(f_ref[0], (H_C, LANES)) + bf_ref[...]), 0.0)
        lfo_ref[0] = lf[0:1]
        r_, c_ = _iota2((LANES, LANES), 0), _iota2((LANES, LANES), 1)
        mcum = ((r_ % H_C == c_ % H_C) & (r_ // H_C <= c_ // H_C)).astype(F32)
        c_row = jnp.dot(lf, mcum, precision=HI, preferred_element_type=F32)[0:1]
        diag = jnp.where(_iota2((R, LANES), 0) == _iota2((R, LANES), 1),
                         jnp.broadcast_to(c_row, (R, LANES)), 0.0)
        cnq = jnp.dot(diag, jnp.ones((LANES, LANES), F32), precision=HI, preferred_element_type=F32)
        cnq_ref[...] = cnq
        zpad = jnp.zeros((LANES - R, hd), F32)
        kn = jnp.concatenate([kn_ref[0], zpad], axis=0).astype(BF16)
        vn = jnp.concatenate([vn_ref[0], zpad], axis=0).astype(BF16)
        s = lax.dot_general(q_ref[0].astype(BF16), kn, (((1,), (1,)), ((), ())),
                            preferred_element_type=F32) * scale
        s = s + cnq - c_row
        row, col = _iota2((R, LANES), 0), _iota2((R, LANES), 1)
        ok = (row % H_C == col % H_C) & (col // H_C <= row // H_C)
        s = jnp.where(ok, s, -jnp.inf)
        m_new = jnp.max(s, axis=1, keepdims=True)
        p = jnp.exp(s - m_new)
        l_ref[...] = jnp.broadcast_to(jnp.sum(p, axis=1, keepdims=True), (R, LANES))
        acc_ref[...] = jnp.dot(p.astype(BF16), vn, preferred_element_type=F32)
        m_ref[...] = jnp.broadcast_to(m_new, (R, LANES))

    q = q_ref[0].astype(BF16)
    own = (_iota2((R, PW), 0) % H_C) == (_iota2((R, PW), 1) % H_C)
    cnq_w = rep(cnq_ref[...], PW // LANES)
    ss = []
    for g in range(G):
        s = lax.dot_general(q, k_refs[g][0, 0].astype(BF16), (((1,), (1,)), ((), ())),
                            preferred_element_type=F32) * scale
        ss.append(jnp.where(own, s + cnq_w + dp_ref[0, g], -jnp.inf))
    m_prev = m_ref[...]
    m_cur = jnp.max(ss[0], axis=1, keepdims=True)
    for g in range(1, G):
        m_cur = jnp.maximum(m_cur, jnp.max(ss[g], axis=1, keepdims=True))
    m_new = jnp.maximum(m_prev, m_cur)
    m_w = rep(m_new, PW // LANES)
    lsum, pv = None, None
    for g in range(G):
        p = jnp.exp(ss[g] - m_w)
        ls = jnp.sum(p, axis=1, keepdims=True)
        d = jnp.dot(p.astype(BF16), v_refs[g][0, 0].astype(BF16), preferred_element_type=F32)
        lsum = ls if lsum is None else lsum + ls
        pv = d if pv is None else pv + d
    alpha = jnp.exp(m_prev - m_new)
    l_ref[...] = alpha * l_ref[...] + lsum
    acc_ref[...] = alpha * acc_ref[...] + pv
    m_ref[...] = m_new

    @pl.when(j == nsteps - 1)
    def _():
        o_ref[0] = (acc_ref[...] / l_ref[...]).astype(o_ref.dtype)


def _fox_sample(qm, kn, vn, f_row, bf_row, dp, ck4, cv4, layer, page_table):
    B, R, hd = qm.shape
    tn = R // H_C
    npg = page_table.shape[1]
    G = _pick(npg, (8, 4, 2, 1))
    nsteps = npg // G
    PW = PAGE * H_C
    seq = lambda b, j, pt: (b, 0, 0)
    page = lambda g: (lambda b, j, pt: (layer, pt[b, G * j + g], 0, 0))
    grid_spec = pltpu.PrefetchScalarGridSpec(
        num_scalar_prefetch=1,
        grid=(B, nsteps),
        in_specs=[pl.BlockSpec((1, R, hd), seq), pl.BlockSpec((1, R, hd), seq), pl.BlockSpec((1, R, hd), seq),
                  pl.BlockSpec((1, 1, LANES), seq),
                  pl.BlockSpec((1, LANES), lambda b, j, pt: (0, 0)),
                  pl.BlockSpec((1, G, 1, PW), lambda b, j, pt: (b, j, 0, 0))]
                 + [pl.BlockSpec((1, 1, PW, hd), page(g)) for g in range(G)] * 2,
        out_specs=[pl.BlockSpec((1, R, hd), seq), pl.BlockSpec((1, 1, LANES), seq)],
        scratch_shapes=[pltpu.VMEM((R, LANES), F32), pltpu.VMEM((R, LANES), F32),
                        pltpu.VMEM((R, hd), F32), pltpu.VMEM((R, LANES), F32)])
    return pl.pallas_call(
        functools.partial(_fox_sample_kernel, G=G, tn=tn, nsteps=nsteps, scale=hd ** -0.5),
        grid_spec=grid_spec,
        out_shape=[jax.ShapeDtypeStruct((B, R, hd), BF16), jax.ShapeDtypeStruct((B, 1, LANES), F32)],
        compiler_params=_cp(("parallel", "arbitrary")),
        name="fox_sample",
    )(page_table, qm, kn, vn, f_row, bf_row, dp, *([ck4] * G), *([cv4] * G))


def _gla_kernel(q_ref, k_ref, v_ref, g_ref, ld_ref, aw_ref, ab_ref, ng_ref, s0_ref, y_ref, sout_ref,
                s_ref, o_ref, u_s, sb_s, *, tt, nt, t_valid):
    j = pl.program_id(2)
    dk = q_ref.shape[-1]
    C = GLA_CHUNK
    nch = tt // C if t_valid is None else -(-t_valid // C)

    @pl.when(j == 0)
    def _():
        s_ref[...] = s0_ref[0, 0]

    z = jnp.dot(ld_ref[0], aw_ref[...], precision=HI, preferred_element_type=F32) + ab_ref[...]
    lga = _log_sigmoid(z) * (1.0 / GLA_TAU)
    kx = k_ref[0]
    if t_valid is not None:
        valid = _iota2((tt, dk), 0) < t_valid
        lga = jnp.where(valid, lga, 0.0)
        kx = jnp.where(valid, kx, 0.0)
    same = (_iota2((tt, tt), 0) // C) == (_iota2((tt, tt), 1) // C)
    upto = _iota2((tt, tt), 0) >= _iota2((tt, tt), 1)
    b = jnp.dot((same & upto).astype(F32), lga, precision=HI, preferred_element_type=F32)
    bend = jnp.dot(same.astype(F32), lga, precision=HI, preferred_element_type=F32)
    qx = q_ref[0] * (dk ** -0.5)
    qe = qx * jnp.exp(b)
    kdt = (kx * jnp.exp(bend - b)).T
    kdt_h = kdt.astype(BF16)
    kdt_l = (kdt - kdt_h.astype(F32)).astype(BF16)
    dect = jnp.exp(bend).T
    v = v_ref[0]
    v_h = v.astype(BF16)
    v_l = (v - v_h.astype(F32)).astype(BF16)
    zero = jnp.zeros_like(v_h)
    rowc = _iota2((tt, v.shape[-1]), 0) // C
    tri_rows = _iota2((C, dk), 0)
    for c in range(nch):
        vm_h = jnp.where(rowc == c, v_h, zero)
        vm_l = jnp.where(rowc == c, v_l, zero)
        u_s[c] = (jnp.dot(kdt_h, vm_h, preferred_element_type=F32)
                  + jnp.dot(kdt_l, vm_h, preferred_element_type=F32)
                  + jnp.dot(kdt_h, vm_l, preferred_element_type=F32))
    S = s_ref[...]
    for c in range(nch):
        sb_s[c] = S.astype(BF16)
        S = S * dect[:, C * c:C * c + 1] + u_s[c]
    s_ref[...] = S
    for c in range(nch):
        r0 = C * c
        q_c, k_c, b_c, v_c = qx[r0:r0 + C], kx[r0:r0 + C], b[r0:r0 + C], v[r0:r0 + C]
        o_c = jnp.dot(qe[r0:r0 + C].astype(BF16), sb_s[c], preferred_element_type=F32)
        for s in range(C):
            rel = jnp.exp(jnp.where(tri_rows >= s, b_c - b_c[s:s + 1, :], -jnp.inf))
            att = jnp.sum(q_c * k_c[s:s + 1, :] * rel, axis=1, keepdims=True)
            o_c = o_c + att * v_c[s:s + 1, :]
        o_ref[r0:r0 + C, :] = o_c
    if nch * C < tt:
        o_ref[nch * C:tt, :] = jnp.zeros((tt - nch * C, o_ref.shape[-1]), F32)
    od = o_ref[...]
    od = od * lax.rsqrt(jnp.mean(od * od, axis=-1, keepdims=True) + LN_EPS)
    gate = g_ref[0]
    y_ref[0] = (od * ng_ref[...] * (gate * jax.nn.sigmoid(gate))).astype(y_ref.dtype)

    @pl.when(j == nt - 1)
    def _():
        sout_ref[0, 0] = s_ref[...]


def _gla_mixer(u3, qc, kc, vc, gc, ldc, aw_pad, ab, ng, s0, t_valid=None):
    B, Tp, _ = u3.shape
    _, H, dk, dv = s0.shape
    tt = _pick(Tp, (256, 128))
    nt = Tp // tt if t_valid is None else 1
    return pl.pallas_call(
        functools.partial(_gla_kernel, tt=tt, nt=nt, t_valid=t_valid),
        grid=(B, H, nt),
        in_specs=[pl.BlockSpec((1, tt, dk), lambda b, h, j: (b, j, qc + h)),
                  pl.BlockSpec((1, tt, dk), lambda b, h, j: (b, j, kc + h)),
                  pl.BlockSpec((1, tt, dv), lambda b, h, j: (b, j, vc + h)),
                  pl.BlockSpec((1, tt, dv), lambda b, h, j: (b, j, gc + h)),
                  pl.BlockSpec((1, tt, LANES), lambda b, h, j: (b, j, ldc)),
                  pl.BlockSpec((LANES, dk), lambda b, h, j: (0, h)),
                  pl.BlockSpec((1, dk), lambda b, h, j: (0, h)),
                  pl.BlockSpec((1, dv), lambda b, h, j: (0, h)),
                  pl.BlockSpec((1, 1, dk, dv), lambda b, h, j: (b, h, 0, 0))],
        out_specs=[pl.BlockSpec((1, tt, dv), lambda b, h, j: (b, j, h)),
                   pl.BlockSpec((1, 1, dk, dv), lambda b, h, j: (b, h, 0, 0))],
        out_shape=[jax.ShapeDtypeStruct((B, nt * tt, H * dv), BF16),
                   jax.ShapeDtypeStruct((B, H, dk, dv), F32)],
        scratch_shapes=[pltpu.VMEM((dk, dv), F32), pltpu.VMEM((tt, dv), F32),
                        pltpu.VMEM((tt // GLA_CHUNK, dk, dv), F32), pltpu.VMEM((tt // GLA_CHUNK, dk, dv), BF16)],
        compiler_params=_cp(("parallel", "parallel", "arbitrary")),
        name="gla",
    )(u3, u3, u3, u3, u3, aw_pad, ab.reshape(1, -1), ng.reshape(1, -1), s0)


def _pack_w_in(w, fc, wd, dkq):
    D = w.shape[0]
    d0 = fc + H_C
    ld0 = d0 + 2 * dkq + 2 * wd
    n_ld = w.shape[1] - ld0
    return jnp.concatenate([w[:, :fc], w[:, d0:ld0], w[:, fc:d0], jnp.zeros((D, LANES - H_C), w.dtype),
                            w[:, ld0:], jnp.zeros((D, LANES - n_ld), w.dtype)], axis=1)


def _token_mixing(xb, B, T, P, layer, st, sample):
    M, D = xb.shape
    wa = wb = wc = D // 4
    wd = D - 3 * (D // 4)
    dkq = wd // 2
    hd = wc // H_C
    n_ld = P['gla_aw'].shape[0]
    cB = 2 * wa
    cC = cB + 3 * wb + 2 * LANES
    cD = cC + 3 * wc
    tm = _pick(M, (1024, 32))
    w_in = _pack_w_in(P['w_in'], cD, wd, dkq)
    N = w_in.shape[1]
    u = _matmul(xb, w_in, (), tm, _pick(N, (512,)), D)
    u3 = u.reshape(B, T, N)
    cF = cD + 2 * dkq + 2 * wd
    cL = cF + LANES
    bf_pad = jnp.concatenate([P['fox_bf'], jnp.zeros((LANES - H_C,), F32)])
    aw_pad = jnp.concatenate([P['gla_aw'], jnp.zeros((LANES - n_ld, dkq), F32)], axis=0)
    kc = u3[:, :, cC + wc:cC + 2 * wc].reshape(B, T, H_C, hd)
    vc = u3[:, :, cC + 2 * wc:cC + 3 * wc].reshape(B, T, H_C, hd)
    shift_new = u3[:, T - 1, cB:cB + 3 * wb + 2 * LANES]

    ya, conv_new = _conv_mixer(u3, st['conv'], P['conv_w'], P['conv_b'], P['conv_ln_g'], P['conv_ln_b'])

    if sample is None:
        up, t_valid = u3, None
    else:
        up, t_valid = jnp.pad(u3, ((0, 0), (0, RW_TILE - T), (0, 0))), T
    yb, wkv_new = _rwkv_mixer(up, cB // wb, st['shift'], st['wkv'], P, t_valid)
    yd, gla_new = _gla_mixer(up, cD // (dkq // H_D), (cD + dkq) // (dkq // H_D),
                             (cD + 2 * dkq) // (wd // H_D), (cD + 2 * dkq + wd) // (wd // H_D),
                             cL // LANES, aw_pad, P['gla_ab'], P['gla_ng'], st['gla'], t_valid)
    yb, yd = yb[:, :T], yd[:, :T]

    if sample is None:
        logf_pad, cq, ck = _fox_prep(u3, cF // LANES, bf_pad.reshape(1, LANES))
        logf = logf_pad[:, :, :H_C]
        yc = _fox_prompt(u3, cC // hd, cq, ck, hd)
    else:
        R = T * H_C
        rows = lambda c0: u3[:, :, c0:c0 + wc].reshape(B, R, hd)
        f_row = jnp.pad(u3[:, :, cF:cF + H_C].reshape(B, 1, R), ((0, 0), (0, 0), (0, LANES - R)))
        bf_row = jnp.pad(jnp.tile(P['fox_bf'], T), (0, LANES - R)).reshape(1, LANES)
        dpt = _fox_decay(sample['clf_t'], layer, sample['page_table'])
        dp = dpt.transpose(0, 1, 3, 2).reshape(B, dpt.shape[1], 1, PAGE * H_C)
        yc, lfo = _fox_sample(rows(cC), rows(cC + wc), rows(cC + 2 * wc), f_row, bf_row, dp,
                              sample['ck'], sample['cv'], layer, sample['page_table'])
        yc = yc.reshape(B, T, wc)
        logf = lfo[:, 0, :R].reshape(B, T, H_C)

    ymix = jnp.concatenate([ya, yb, yc, yd], axis=-1).reshape(M, D)
    y = _matmul(ymix, P['w_o_all'], (layer,), tm, _pick(D, (512,)), D)
    return y, (kc, vc, logf, conv_new, shift_new, wkv_new, gla_new)


def _mm_wcopy_kernel(x_ref, w_ref, o_ref, wc_ref):
    wb = w_ref[...].astype(BF16)
    wc_ref[...] = wb
    o_ref[...] = jnp.dot(x_ref[...], wb, preferred_element_type=F32)


def _matmul_wcopy(x, w, widx, tn):
    M, K = x.shape
    N = w.shape[-1]
    return pl.pallas_call(
        _mm_wcopy_kernel,
        grid=(N // tn,),
        in_specs=[pl.BlockSpec((M, K), lambda j: (0, 0)),
                  pl.BlockSpec((None,) * len(widx) + (K, tn), lambda j: widx + (0, j))],
        out_specs=[pl.BlockSpec((M, tn), lambda j: (0, j)), pl.BlockSpec((K, tn), lambda j: (0, j))],
        out_shape=[jax.ShapeDtypeStruct((M, N), F32), jax.ShapeDtypeStruct((K, N), BF16)],
        compiler_params=_cp(("arbitrary",)),
        name="matmul_wcopy",
    )(x, w)


def _swiglu(xb, P, layer, which, wd_bf16):
    M, D = xb.shape
    F = P['ffn_wg_all'].shape[-1]
    widx = (layer, which)
    h = _ffn_up(xb, P['ffn_wg_all'], P['ffn_wu_all'], widx, _pick(M, (1024, 32)), _pick(F, (256,)))
    if wd_bf16 is None:
        return _matmul_wcopy(h, P['ffn_wd_all'], widx, _pick(D, (256,)))
    return _matmul(h, wd_bf16, (), _pick(M, (512, 32)), _pick(D, (256,)), F), None


def _layer(x, xb, B, T, P, layer, st, sample, alpha, wd_bf16):
    y, wd0 = _swiglu(xb, P, layer, 0, wd_bf16[0])
    x, xb = _add_ln(x, y, P['ln_g'][0], P['ln_b'][0], alpha, 0.5)
    y, new = _token_mixing(xb, B, T, P, layer, st, sample)
    x, xb = _add_ln(x, y, P['ln_g'][1], P['ln_b'][1], alpha, 1.0)
    y, wd1 = _swiglu(xb, P, layer, 1, wd_bf16[1])
    x, xb = _add_ln(x, y, P['ln_g'][2], P['ln_b'][2], alpha, 0.5)
    return x, xb, new, (wd0, wd1)


def kernel(x_prompt, x_sample, cache_k, cache_v, cache_logf, state_conv, state_shift, state_wkv, state_gla, page_table, ln_g, ln_b, ffn_wg, ffn_wu, ffn_wd, w_in, w_o, conv_w, conv_b, conv_ln_g, conv_ln_b, rwkv_mu, rwkv_w0, rwkv_w2, rwkv_a0, rwkv_a2, rwkv_g2, rwkv_kk, rwkv_ka, rwkv_rk, rwkv_gn_g, rwkv_gn_b, fox_bf, gla_aw, gla_ab, gla_ng):
    big = dict(ffn_wg_all=ffn_wg, ffn_wu_all=ffn_wu, ffn_wd_all=ffn_wd, w_o_all=w_o)
    stacked = dict(ln_g=ln_g, ln_b=ln_b, w_in=w_in, conv_w=conv_w, conv_b=conv_b, conv_ln_g=conv_ln_g,
                   conv_ln_b=conv_ln_b, rwkv_mu=rwkv_mu, rwkv_w0=rwkv_w0, rwkv_w2=rwkv_w2, rwkv_a0=rwkv_a0,
                   rwkv_a2=rwkv_a2, rwkv_g2=rwkv_g2, rwkv_kk=rwkv_kk, rwkv_ka=rwkv_ka, rwkv_rk=rwkv_rk,
                   rwkv_gn_g=rwkv_gn_g, rwkv_gn_b=rwkv_gn_b, fox_bf=fox_bf, gla_aw=gla_aw, gla_ab=gla_ab,
                   gla_ng=gla_ng)
    depth = w_in.shape[0]
    Bp, Tp, D = x_prompt.shape
    Bs, Ts, _ = x_sample.shape
    alpha = (2.0 * depth) ** 0.25
    n_pool, page, hc, hdc = cache_k.shape[1:]
    ck = cache_k.reshape(depth, n_pool, page * hc, hdc)
    cv = cache_v.reshape(depth, n_pool, page * hc, hdc)
    clf_t = cache_logf.transpose(0, 1, 3, 2)
    xp, xs = x_prompt.reshape(Bp * Tp, D), x_sample.reshape(Bs * Ts, D)
    xpb, xsb = xp.astype(BF16), xs.astype(BF16)
    zero_state = dict(conv=jnp.zeros((Bp,) + state_conv.shape[2:], F32),
                      shift=jnp.zeros((Bp,) + state_shift.shape[2:], F32),
                      wkv=jnp.zeros((Bp,) + state_wkv.shape[2:], F32),
                      gla=jnp.zeros((Bp,) + state_gla.shape[2:], F32))
    st_p, st_s = [], []
    for l in range(depth):
        P = {n: a[l] for n, a in stacked.items()}
        P.update(big)
        st = dict(conv=state_conv[l], shift=state_shift[l], wkv=state_wkv[l], gla=state_gla[l])
        xs, xsb, s, wd_bf16 = _layer(xs, xsb, Bs, Ts, P, l, st,
                                     dict(ck=ck, cv=cv, clf_t=clf_t, page_table=page_table), alpha, (None, None))
        st_s.append(s)
        xp, xpb, s, _ = _layer(xp, xpb, Bp, Tp, P, l, zero_state, None, alpha, wd_bf16)
        st_p.append(s)
    stk = lambda sts, i: jnp.stack([s[i] for s in sts])
    return ((xp.reshape(Bp, Tp, D), xs.reshape(Bs, Ts, D))
            + tuple(stk(st_p, i) for i in range(7)) + tuple(stk(st_s, i) for i in range(7)))
```
